```python
import jax, jax.numpy as jnp
from jax import lax
import numpy as np

D_MODEL = 2048
BATCH = 4
SEQ = 2048
DEPTH = 2

CHUNK = 64
EPS = 1e-6
D_POOL = D_MODEL // 2
POOL_WINDOWS = (2, 4, 8, 16)
N_POOL_GROUPS = len(POOL_WINDOWS)
POOL_GROUP = D_POOL // N_POOL_GROUPS
D_CONV = D_MODEL // 2
CONV_K = 31
D_AB_IN = D_POOL + 2 * D_CONV
D_SHORT = D_MODEL
SHORT_K = 3
D_FF = 4 * D_MODEL

N_EVEN = (DEPTH + 1) // 2
N_ODD = DEPTH // 2

kernel_name = "hybrid_pool_conformer_shortconv_trunk"


def rms_norm(x, g):
    xf = x.astype(jnp.float32)
    y = xf * lax.rsqrt(jnp.mean(xf * xf, axis=-1, keepdims=True) + EPS)
    return (y * g.astype(jnp.float32)).astype(x.dtype)


def layer_norm(x, g, b):
    xf = x.astype(jnp.float32)
    mu = jnp.mean(xf, axis=-1, keepdims=True)
    xc = xf - mu
    var = jnp.mean(xc * xc, axis=-1, keepdims=True)
    y = xc * lax.rsqrt(var + EPS) * g.astype(jnp.float32) + b.astype(jnp.float32)
    return y.astype(x.dtype)


def causal_depthwise_conv(u, w):
    k = w.shape[0]
    return lax.conv_general_dilated(
        u, w[:, None, :].astype(u.dtype), window_strides=(1,), padding=[(k - 1, 0)],
        dimension_numbers=("NWC", "WIO", "NWC"), feature_group_count=u.shape[-1])


def multiscale_pool(u, pool_w, pool_scale):
    b, t, _ = u.shape
    uf = u.astype(jnp.float32)
    csp = jnp.pad(jnp.cumsum(uf, axis=1), ((0, 0), (1, 0), (0, 0)))
    n_valid = jnp.arange(1, t + 1, dtype=jnp.float32)
    means = []
    for g, w in enumerate(POOL_WINDOWS):
        c = csp[..., g * POOL_GROUP:(g + 1) * POOL_GROUP]
        lag = jnp.pad(c, ((0, 0), (w - 1, 0), (0, 0)))[:, :t]
        cnt = jnp.minimum(n_valid, float(w))[None, :, None]
        means.append((c[:, 1:] - lag) / cnt)
    pooled = (jnp.concatenate(means, axis=-1) - uf).astype(u.dtype)
    pooled = pooled.reshape(b, t, N_POOL_GROUPS, POOL_GROUP)
    mixed = jnp.einsum("btgc,gce->btge", pooled, pool_w).reshape(b, t, D_POOL)
    return mixed * pool_scale


def pool_conformer_mixer(h, w_in, pool_w, pool_scale, conv_w, conv_b, ln_g, ln_b, w_out):
    z = jnp.einsum("btd,de->bte", h, w_in)
    u_pool = z[..., :D_POOL]
    v = z[..., D_POOL:D_POOL + D_CONV]
    gate = z[..., D_POOL + D_CONV:]
    y_pool = multiscale_pool(u_pool, pool_w, pool_scale)
    c = causal_depthwise_conv(v * jax.nn.sigmoid(gate), conv_w) + conv_b
    y_conv = jax.nn.silu(layer_norm(c, ln_g, ln_b))
    y = jnp.concatenate([y_pool, y_conv], axis=-1)
    return jnp.einsum("bte,ed->btd", y, w_out)


def short_conv_mixer(h, w_in, conv_w, w_out):
    z = jnp.einsum("btd,de->bte", h, w_in)
    b_gate = z[..., :D_SHORT]
    c_gate = z[..., D_SHORT:2 * D_SHORT]
    u = z[..., 2 * D_SHORT:]
    y = b_gate * causal_depthwise_conv(c_gate * u, conv_w)
    return jnp.einsum("bte,ed->btd", y, w_out)


def sq_relu_mlp(h, w1, w2):
    a = jax.nn.relu(jnp.einsum("btd,df->btf", h, w1))
    return jnp.einsum("btf,fd->btd", a * a, w2)


def setup_inputs(seed: int = 0) -> dict:
    key = jax.random.key(seed)
    ks = jax.random.split(key, 20)
    f32 = jnp.float32

    def nrm(k, shape, scale):
        return jax.random.normal(k, shape, f32) * scale

    def gain(k, shape):
        return 1.0 + 0.02 * jax.random.normal(k, shape, f32)

    return {
        "x": jax.random.normal(ks[0], (BATCH, SEQ, D_MODEL), f32),
        "mix_pre_g": gain(ks[1], (DEPTH, D_MODEL)),
        "mix_post_g": gain(ks[2], (DEPTH, D_MODEL)),
        "ffn_pre_g": gain(ks[3], (DEPTH, D_MODEL)),
        "ffn_post_g": gain(ks[4], (DEPTH, D_MODEL)),
        "ab_w_in": nrm(ks[5], (N_EVEN, D_MODEL, D_AB_IN), D_MODEL ** -0.5),
        "pool_w": nrm(ks[6], (N_EVEN, N_POOL_GROUPS, POOL_GROUP, POOL_GROUP), POOL_GROUP ** -0.5),
        "pool_scale": 1.0 + 0.1 * jax.random.normal(ks[7], (N_EVEN, D_POOL), f32),
        "conv_w": nrm(ks[8], (N_EVEN, CONV_K, D_CONV), CONV_K ** -0.5),
        "conv_b": nrm(ks[9], (N_EVEN, D_CONV), 0.02),
        "conv_ln_g": gain(ks[10], (N_EVEN, D_CONV)),
        "conv_ln_b": nrm(ks[11], (N_EVEN, D_CONV), 0.02),
        "ab_w_out": nrm(ks[12], (N_EVEN, D_POOL + D_CONV, D_MODEL), (D_POOL + D_CONV) ** -0.5),
        "sc_w_in": nrm(ks[13], (N_ODD, D_MODEL, 3 * D_SHORT), D_MODEL ** -0.5),
        "sc_conv_w": nrm(ks[14], (N_ODD, SHORT_K, D_SHORT), SHORT_K ** -0.5),
        "sc_w_out": nrm(ks[15], (N_ODD, D_SHORT, D_MODEL), D_SHORT ** -0.5),
        "ffn_w1": nrm(ks[16], (DEPTH, D_MODEL, D_FF), D_MODEL ** -0.5),
        "ffn_w2": nrm(ks[17], (DEPTH, D_FF, D_MODEL), D_FF ** -0.5),
    }


def reference(x, mix_pre_g, mix_post_g, ffn_pre_g, ffn_post_g, ab_w_in, pool_w, pool_scale,
              conv_w, conv_b, conv_ln_g, conv_ln_b, ab_w_out, sc_w_in, sc_conv_w, sc_w_out,
              ffn_w1, ffn_w2):
    for layer in range(DEPTH):
        i = layer // 2
        h = rms_norm(x, mix_pre_g[layer])
        if layer % 2 == 0:
            m = pool_conformer_mixer(h, ab_w_in[i], pool_w[i], pool_scale[i], conv_w[i], conv_b[i],
                                     conv_ln_g[i], conv_ln_b[i], ab_w_out[i])
        else:
            m = short_conv_mixer(h, sc_w_in[i], sc_conv_w[i], sc_w_out[i])
        x = x + rms_norm(m, mix_post_g[layer])
        h = rms_norm(x, ffn_pre_g[layer])
        x = x + rms_norm(sq_relu_mlp(h, ffn_w1[layer], ffn_w2[layer]), ffn_post_g[layer])
    return x
```

```python
import functools

import jax
import jax.numpy as jnp
from jax import lax
from jax.experimental import pallas as pl
from jax.experimental.pallas import tpu as pltpu

EPS = 1e-6
POOL_WINDOWS = (2, 4, 8, 16)

V7X_SUBLANES = 8
V7X_VMEM_LIMIT_CAP = 60000 * 1024

F32 = jnp.float32
BF16 = jnp.bfloat16


def _halo_rows(n):
    return -(-n // V7X_SUBLANES) * V7X_SUBLANES


def _params(semantics, vmem_bytes):
    return pltpu.CompilerParams(
        dimension_semantics=semantics,
        vmem_limit_bytes=min(int(vmem_bytes), V7X_VMEM_LIMIT_CAP))


def _const_spec(shape, index_map):
    return pl.BlockSpec(shape, index_map, pipeline_mode=pl.Buffered(1))


def _rms(x, g):
    ms = jnp.mean(x * x, axis=-1, keepdims=True)
    return x * lax.rsqrt(ms + EPS) * g


def _rms_kernel(x_ref, g_ref, o_ref):
    o_ref[...] = _rms(x_ref[...], g_ref[...]).astype(o_ref.dtype)


def _rms_cast(x, g, *, bm):
    m, d = x.shape
    return pl.pallas_call(
        _rms_kernel,
        grid=(m // bm,),
        in_specs=[pl.BlockSpec((bm, d), lambda i: (i, 0)),
                  _const_spec((1, d), lambda i: (0, 0))],
        out_specs=pl.BlockSpec((bm, d), lambda i: (i, 0)),
        out_shape=jax.ShapeDtypeStruct((m, d), BF16),
        compiler_params=_params(("arbitrary",), 2 * bm * d * (4 + 2) + 4 * bm * d * 4),
        name="rms_cast",
    )(x, g.reshape(1, d))


def _load_history(buf, i, *, bm, seq, halo):
    first = lax.rem(i * bm, seq) == 0

    @pl.when(first)
    def _():
        buf[0:halo, :] = jnp.zeros((halo, buf.shape[1]), buf.dtype)

    @pl.when(jnp.logical_not(first))
    def _():
        buf[0:halo, :] = buf[bm:bm + halo, :]


def _pool_kernel(h_ref, w_ref, pw_ref, ps_ref, o_ref, wbf, pwbf, ubuf, *, bm, seq, halo, group):
    i = pl.program_id(0)

    @pl.when(i == 0)
    def _():
        wbf[...] = w_ref[...].astype(BF16)
        pwbf[...] = pw_ref[...].astype(BF16)

    _load_history(ubuf, i, bm=bm, seq=seq, halo=halo)
    ubuf[halo:, :] = jnp.dot(h_ref[...], wbf[...], preferred_element_type=F32)

    pos = lax.rem(i * bm, seq) + lax.broadcasted_iota(jnp.int32, (bm, 1), 0)
    for g, w in enumerate(POOL_WINDOWS):
        cols = slice(g * group, (g + 1) * group)
        u = ubuf[halo:halo + bm, cols]
        s = u
        for d in range(1, w):
            s = s + ubuf[halo - d:halo - d + bm, cols]
        cnt = jnp.minimum(pos + 1, w).astype(F32)
        pooled = s / cnt - u
        mixed = jnp.dot(pooled.astype(BF16), pwbf[g], preferred_element_type=F32)
        o_ref[:, cols] = (mixed * ps_ref[:, cols]).astype(o_ref.dtype)


def _pool_branch(h, w_in, pool_w, pool_scale, *, bm, seq):
    m, d = h.shape
    n_groups, group, _ = pool_w.shape
    d_pool = n_groups * group
    halo = _halo_rows(max(POOL_WINDOWS) - 1)
    kern = functools.partial(_pool_kernel, bm=bm, seq=seq, halo=halo, group=group)
    vmem = (d * d_pool * (4 + 2) + 2 * bm * d * 2 + 2 * bm * d_pool * 2
            + (halo + bm) * d_pool * 4 + 3 * bm * d_pool * 4 + pool_w.size * 6)
    return pl.pallas_call(
        kern,
        grid=(m // bm,),
        in_specs=[pl.BlockSpec((bm, d), lambda i: (i, 0)),
                  _const_spec((d, d_pool), lambda i: (0, 0)),
                  _const_spec(pool_w.shape, lambda i: (0, 0, 0)),
                  _const_spec((1, d_pool), lambda i: (0, 0))],
        out_specs=pl.BlockSpec((bm, d_pool), lambda i: (i, 0)),
        out_shape=jax.ShapeDtypeStruct((m, d_pool), BF16),
        scratch_shapes=[pltpu.VMEM((d, d_pool), BF16),
                        pltpu.VMEM(pool_w.shape, BF16),
                        pltpu.VMEM((halo + bm, d_pool), F32)],
        compiler_params=_params(("arbitrary",), vmem),
        name="pool_branch",
    )(h, w_in, pool_w, pool_scale.reshape(1, d_pool))


def _glu_conv_kernel(h_ref, wv_ref, wg_ref, cw_ref, cb_ref, o_ref, wvbf, wgbf, gbuf,
                     *, bm, seq, halo, taps, rows):
    i = pl.program_id(1)

    @pl.when(i == 0)
    def _():
        wvbf[...] = wv_ref[...].astype(BF16)
        wgbf[...] = wg_ref[...].astype(BF16)

    _load_history(gbuf, i, bm=bm, seq=seq, halo=halo)
    h = h_ref[...]
    v = jnp.dot(h, wvbf[...], preferred_element_type=F32)
    gate = jnp.dot(h, wgbf[...], preferred_element_type=F32)
    gbuf[halo:, :] = v * jax.nn.sigmoid(gate)

    cw = cw_ref[...]
    cb = cb_ref[...]
    for r0 in range(0, bm, rows):
        acc = jnp.broadcast_to(cb, (rows, cb.shape[1]))
        for k in range(taps):
            start = halo + r0 - (taps - 1) + k
            acc = acc + cw[k:k + 1, :] * gbuf[start:start + rows, :]
        o_ref[r0:r0 + rows, :] = acc


def _glu_conv_branch(h, w_in, conv_w, conv_b, *, d_pool, bm, bn, seq):
    m, d = h.shape
    taps, d_conv = conv_w.shape
    halo = _halo_rows(taps - 1)
    v_blk0 = d_pool // bn
    g_blk0 = (d_pool + d_conv) // bn
    kern = functools.partial(_glu_conv_kernel, bm=bm, seq=seq, halo=halo, taps=taps, rows=32)
    vmem = (2 * 2 * d * bn * 4 + 2 * d * bn * 2 + 2 * bm * d * 2 + 2 * bm * bn * 4
            + (halo + bm) * bn * 4 + 4 * bm * bn * 4)
    return pl.pallas_call(
        kern,
        grid=(d_conv // bn, m // bm),
        in_specs=[pl.BlockSpec((bm, d), lambda j, i: (i, 0)),
                  pl.BlockSpec((d, bn), lambda j, i: (0, v_blk0 + j)),
                  pl.BlockSpec((d, bn), lambda j, i: (0, g_blk0 + j)),
                  pl.BlockSpec((taps, bn), lambda j, i: (0, j)),
                  pl.BlockSpec((1, bn), lambda j, i: (0, j))],
        out_specs=pl.BlockSpec((bm, bn), lambda j, i: (i, j)),
        out_shape=jax.ShapeDtypeStruct((m, d_conv), F32),
        scratch_shapes=[pltpu.VMEM((d, bn), BF16),
                        pltpu.VMEM((d, bn), BF16),
                        pltpu.VMEM((halo + bm, bn), F32)],
        compiler_params=_params(("arbitrary", "arbitrary"), vmem),
        name="glu_conv_branch",
    )(h, w_in, w_in, conv_w, conv_b.reshape(1, d_conv))


def _residual_norms(m, x_ref, gpost_ref, gnext_ref, xo_ref, ho_ref):
    x_new = x_ref[...] + _rms(m, gpost_ref[...])
    xo_ref[...] = x_new
    if ho_ref is not None:
        ho_ref[...] = _rms(x_new, gnext_ref[...]).astype(ho_ref.dtype)


def _out_proj_l0_kernel(yp_ref, c_ref, x_ref, w_ref, lng_ref, lnb_ref, gpost_ref, gnext_ref,
                        xo_ref, ho_ref, wbf, *, d_pool):
    @pl.when(pl.program_id(0) == 0)
    def _():
        wbf[...] = w_ref[...].astype(BF16)

    c = c_ref[...]
    mu = jnp.mean(c, axis=-1, keepdims=True)
    cc = c - mu
    var = jnp.mean(cc * cc, axis=-1, keepdims=True)
    ln = cc * lax.rsqrt(var + EPS) * lng_ref[...] + lnb_ref[...]
    y_conv = (ln * jax.nn.sigmoid(ln)).astype(BF16)
    m = jnp.dot(yp_ref[...], wbf[0:d_pool, :], preferred_element_type=F32)
    m = m + jnp.dot(y_conv, wbf[d_pool:, :], preferred_element_type=F32)
    _residual_norms(m, x_ref, gpost_ref, gnext_ref, xo_ref, ho_ref)


def _out_proj_l1_kernel(y_ref, x_ref, w_ref, gpost_ref, gnext_ref, xo_ref, ho_ref, wbf):
    @pl.when(pl.program_id(0) == 0)
    def _():
        wbf[...] = w_ref[...].astype(BF16)

    m = jnp.dot(y_ref[...], wbf[...], preferred_element_type=F32)
    _residual_norms(m, x_ref, gpost_ref, gnext_ref, xo_ref, ho_ref)


def _out_proj(ys, x, w_out, g_post, g_next, ln=None, *, bm):
    m, d = x.shape
    k = w_out.shape[0]
    row = lambda i: (i, 0)
    fixed = lambda i: (0, 0)
    vec = lambda a: a.reshape(1, -1)
    if ln is None:
        (y,) = ys
        kern = _out_proj_l1_kernel
        ins = [y, x, w_out, vec(g_post), vec(g_next)]
        in_specs = [pl.BlockSpec((bm, k), row), pl.BlockSpec((bm, d), row),
                    _const_spec((k, d), fixed), _const_spec((1, d), fixed), _const_spec((1, d), fixed)]
        tile_bytes = bm * k * 2
    else:
        y_pool, c = ys
        d_pool, d_conv = y_pool.shape[1], c.shape[1]
        kern = functools.partial(_out_proj_l0_kernel, d_pool=d_pool)
        ins = [y_pool, c, x, w_out, vec(ln[0]), vec(ln[1]), vec(g_post), vec(g_next)]
        in_specs = [pl.BlockSpec((bm, d_pool), row), pl.BlockSpec((bm, d_conv), row),
                    pl.BlockSpec((bm, d), row), _const_spec((k, d), fixed),
                    _const_spec((1, d_conv), fixed), _const_spec((1, d_conv), fixed),
                    _const_spec((1, d), fixed), _const_spec((1, d), fixed)]
        tile_bytes = bm * d_pool * 2 + bm * d_conv * 4
    vmem = (k * d * (4 + 2) + 2 * tile_bytes + 2 * bm * d * (4 + 4 + 2) + 4 * bm * d * 4)
    return pl.pallas_call(
        kern,
        grid=(m // bm,),
        in_specs=in_specs,
        out_specs=[pl.BlockSpec((bm, d), row), pl.BlockSpec((bm, d), row)],
        out_shape=[jax.ShapeDtypeStruct((m, d), F32), jax.ShapeDtypeStruct((m, d), BF16)],
        scratch_shapes=[pltpu.VMEM((k, d), BF16)],
        compiler_params=_params(("arbitrary",), vmem),
        name="out_proj_l0" if ln is not None else "out_proj_l1",
    )(*ins)


def _ffn_up_kernel(h_ref, w_ref, o_ref, wbf):
    @pl.when(pl.program_id(1) == 0)
    def _():
        wbf[...] = w_ref[...].astype(BF16)

    a = jnp.maximum(jnp.dot(h_ref[...], wbf[...], preferred_element_type=F32), 0.0)
    o_ref[...] = (a * a).astype(o_ref.dtype)


def _ffn_up(h, w1, layer, *, bm, bn):
    m, d = h.shape
    f = w1.shape[2]
    vmem = 2 * d * bn * 4 + d * bn * 2 + 2 * bm * d * 2 + 2 * bm * bn * 2 + 3 * bm * bn * 4
    return pl.pallas_call(
        _ffn_up_kernel,
        grid=(f // bn, m // bm),
        in_specs=[pl.BlockSpec((bm, d), lambda j, i: (i, 0)),
                  pl.BlockSpec((None, d, bn), lambda j, i: (layer, 0, j))],
        out_specs=pl.BlockSpec((bm, bn), lambda j, i: (i, j)),
        out_shape=jax.ShapeDtypeStruct((m, f), BF16),
        scratch_shapes=[pltpu.VMEM((d, bn), BF16)],
        compiler_params=_params(("arbitrary", "arbitrary"), vmem),
        name="ffn_up",
    )(h, w1)


def _ffn_down_kernel(a_ref, w_ref, x_ref, gpost_ref, gnext_ref, xo_ref, *rest, with_next):
    ho_ref = rest[0] if with_next else None
    k = pl.program_id(1)
    part = jnp.dot(a_ref[...], w_ref[...].astype(BF16), preferred_element_type=F32)

    @pl.when(k == 0)
    def _():
        xo_ref[...] = part

    @pl.when(k != 0)
    def _():
        xo_ref[...] += part

    @pl.when(k == pl.num_programs(1) - 1)
    def _():
        _residual_norms(xo_ref[...], x_ref, gpost_ref, gnext_ref, xo_ref, ho_ref)


def _ffn_down(a, w2, layer, x, g_post, g_next, *, bm, bk):
    m, f = a.shape
    d = w2.shape[2]
    with_next = g_next is not None
    if not with_next:
        g_next = g_post
    row = lambda i, k: (i, 0)
    fixed = lambda i, k: (0, 0)
    out_specs = [pl.BlockSpec((bm, d), row)]
    out_shape = [jax.ShapeDtypeStruct((m, d), F32)]
    if with_next:
        out_specs.append(pl.BlockSpec((bm, d), row))
        out_shape.append(jax.ShapeDtypeStruct((m, d), BF16))
    vmem = (2 * bm * bk * 2 + 2 * bk * d * 4 + bk * d * 2 + 2 * bm * d * (4 + 4 + 2) + 2 * bm * d * 4)
    outs = pl.pallas_call(
        functools.partial(_ffn_down_kernel, with_next=with_next),
        grid=(m // bm, f // bk),
        in_specs=[pl.BlockSpec((bm, bk), lambda i, k: (i, k)),
                  pl.BlockSpec((None, bk, d), lambda i, k: (layer, k, 0)),
                  pl.BlockSpec((bm, d), row),
                  _const_spec((1, d), fixed), _const_spec((1, d), fixed)],
        out_specs=out_specs,
        out_shape=out_shape,
        compiler_params=_params(("arbitrary", "arbitrary"), vmem),
        name="ffn_down",
    )(a, w2, x, g_post.reshape(1, d), g_next.reshape(1, d))
    return outs if with_next else (outs[0], None)


def _gated_conv_kernel(h_ref, wb_ref, wc_ref, wu_ref, cw_ref, o_ref, wbbf, wcbf, wubf, pbuf,
                       *, bm, seq, halo, taps):
    i = pl.program_id(1)

    @pl.when(i == 0)
    def _():
        wbbf[...] = wb_ref[...].astype(BF16)
        wcbf[...] = wc_ref[...].astype(BF16)
        wubf[...] = wu_ref[...].astype(BF16)

    _load_history(pbuf, i, bm=bm, seq=seq, halo=halo)
    h = h_ref[...]
    zc = jnp.dot(h, wcbf[...], preferred_element_type=F32)
    zu = jnp.dot(h, wubf[...], preferred_element_type=F32)
    pbuf[halo:, :] = zc * zu
    cw = cw_ref[...]
    conv = cw[taps - 1:taps, :] * pbuf[halo:halo + bm, :]
    for k in range(taps - 1):
        start = halo - (taps - 1) + k
        conv = conv + cw[k:k + 1, :] * pbuf[start:start + bm, :]
    zb = jnp.dot(h, wbbf[...], preferred_element_type=F32)
    o_ref[...] = (zb * conv).astype(o_ref.dtype)


def _gated_conv_branch(h, w_in, conv_w, *, bm, bn, seq):
    m, d = h.shape
    taps, d_short = conv_w.shape
    halo = _halo_rows(taps - 1)
    nblk = d_short // bn
    kern = functools.partial(_gated_conv_kernel, bm=bm, seq=seq, halo=halo, taps=taps)
    vmem = (3 * 2 * d * bn * 4 + 3 * d * bn * 2 + 2 * bm * d * 2 + 2 * bm * bn * 2
            + (halo + bm) * bn * 4 + 5 * bm * bn * 4)
    return pl.pallas_call(
        kern,
        grid=(nblk, m // bm),
        in_specs=[pl.BlockSpec((bm, d), lambda j, i: (i, 0)),
                  pl.BlockSpec((d, bn), lambda j, i: (0, j)),
                  pl.BlockSpec((d, bn), lambda j, i: (0, nblk + j)),
                  pl.BlockSpec((d, bn), lambda j, i: (0, 2 * nblk + j)),
                  pl.BlockSpec((taps, bn), lambda j, i: (0, j))],
        out_specs=pl.BlockSpec((bm, bn), lambda j, i: (i, j)),
        out_shape=jax.ShapeDtypeStruct((m, d_short), BF16),
        scratch_shapes=[pltpu.VMEM((d, bn), BF16), pltpu.VMEM((d, bn), BF16),
                        pltpu.VMEM((d, bn), BF16), pltpu.VMEM((halo + bm, bn), F32)],
        compiler_params=_params(("arbitrary", "arbitrary"), vmem),
        name="gated_conv_branch",
    )(h, w_in, w_in, w_in, conv_w)


def kernel(x, mix_pre_g, mix_post_g, ffn_pre_g, ffn_post_g, ab_w_in, pool_w, pool_scale,
           conv_w, conv_b, conv_ln_g, conv_ln_b, ab_w_out, sc_w_in, sc_conv_w, sc_w_out,
           ffn_w1, ffn_w2):
    batch, seq, d = x.shape
    depth = mix_pre_g.shape[0]
    d_pool = pool_scale.shape[1]
    xs = x.reshape(batch * seq, d)
    h = _rms_cast(xs, mix_pre_g[0], bm=512)
    for layer in range(depth):
        i = layer // 2
        if layer % 2 == 0:
            y_pool = _pool_branch(h, ab_w_in[i], pool_w[i], pool_scale[i], bm=512, seq=seq)
            c = _glu_conv_branch(h, ab_w_in[i], conv_w[i], conv_b[i],
                                 d_pool=d_pool, bm=1024, bn=512, seq=seq)
            xs, h = _out_proj((y_pool, c), xs, ab_w_out[i], mix_post_g[layer], ffn_pre_g[layer],
                              ln=(conv_ln_g[i], conv_ln_b[i]), bm=256)
        else:
            y = _gated_conv_branch(h, sc_w_in[i], sc_conv_w[i], bm=1024, bn=512, seq=seq)
            xs, h = _out_proj((y,), xs, sc_w_out[i], mix_post_g[layer], ffn_pre_g[layer], bm=256)
        a = _ffn_up(h, ffn_w1, layer, bm=1024, bn=1024)
        g_next = mix_pre_g[layer + 1] if layer + 1 < depth else None
        xs, h = _ffn_down(a, ffn_w2, layer, xs, ffn_post_g[layer], g_next, bm=512, bk=512)
    return xs.reshape(batch, seq, d)
```

```python
import functools

import jax
import jax.numpy as jnp
from jax import lax
from jax.experimental import pallas as pl
from jax.experimental.pallas import tpu as pltpu

EPS = 1e-6
POOL_WINDOWS = (2, 4, 8, 16)

V7X_SUBLANES = 8
V7X_LANES = 128
V7X_VMEM_LIMIT_CAP = 60000 * 1024

F32 = jnp.float32
BF16 = jnp.bfloat16


def _halo_rows(n):
    return -(-n // V7X_SUBLANES) * V7X_SUBLANES


def _params(semantics, vmem_bytes):
    return pltpu.CompilerParams(
        dimension_semantics=semantics,
        vmem_limit_bytes=min(int(vmem_bytes), V7X_VMEM_LIMIT_CAP))


def _const_spec(shape, index_map):
    return pl.BlockSpec(shape, index_map, pipeline_mode=pl.Buffered(1))


def _rms(x, g):
    ms = jnp.mean(x * x, axis=-1, keepdims=True)
    return x * lax.rsqrt(ms + EPS) * g


def _rms_kernel(x_ref, g_ref, o_ref):
    o_ref[...] = _rms(x_ref[...], g_ref[...]).astype(o_ref.dtype)


def _rms_cast(x, g, *, bm):
    m, d = x.shape
    return pl.pallas_call(
        _rms_kernel,
        grid=(m // bm,),
        in_specs=[pl.BlockSpec((bm, d), lambda i: (i, 0)),
                  _const_spec((1, d), lambda i: (0, 0))],
        out_specs=pl.BlockSpec((bm, d), lambda i: (i, 0)),
        out_shape=jax.ShapeDtypeStruct((m, d), BF16),
        compiler_params=_params(("arbitrary",), 2 * bm * d * (4 + 2) + 4 * bm * d * 4),
        name="rms_cast",
    )(x, g.reshape(1, d))


def _load_history(buf, i, *, bm, seq, halo):
    first = lax.rem(i * bm, seq) == 0

    @pl.when(first)
    def _():
        buf[0:halo, :] = jnp.zeros((halo, buf.shape[1]), buf.dtype)

    @pl.when(jnp.logical_not(first))
    def _():
        buf[0:halo, :] = buf[bm:bm + halo, :]


def _pool_kernel(h_ref, w_ref, pw_ref, ps_ref, o_ref, wbf, pwbf, ubuf, *, bm, seq, halo, group):
    i = pl.program_id(0)

    @pl.when(i == 0)
    def _():
        wbf[...] = w_ref[...].astype(BF16)
        pwbf[...] = pw_ref[...].astype(BF16)

    _load_history(ubuf, i, bm=bm, seq=seq, halo=halo)
    ubuf[halo:, :] = jnp.dot(h_ref[...], wbf[...], preferred_element_type=F32)

    pos = lax.rem(i * bm, seq) + lax.broadcasted_iota(jnp.int32, (bm, 1), 0)
    for g, w in enumerate(POOL_WINDOWS):
        cols = slice(g * group, (g + 1) * group)
        u = ubuf[halo:halo + bm, cols]
        s = u
        for d in range(1, w):
            s = s + ubuf[halo - d:halo - d + bm, cols]
        cnt = jnp.minimum(pos + 1, w).astype(F32)
        pooled = s / cnt - u
        mixed = jnp.dot(pooled.astype(BF16), pwbf[g], preferred_element_type=F32)
        o_ref[:, cols] = (mixed * ps_ref[:, cols]).astype(o_ref.dtype)


def _pool_branch(h, w_in, pool_w, pool_scale, *, bm, seq):
    m, d = h.shape
    n_groups, group, _ = pool_w.shape
    d_pool = n_groups * group
    halo = _halo_rows(max(POOL_WINDOWS) - 1)
    kern = functools.partial(_pool_kernel, bm=bm, seq=seq, halo=halo, group=group)
    vmem = (d * d_pool * (4 + 2) + 2 * bm * d * 2 + 2 * bm * d_pool * 2
            + (halo + bm) * d_pool * 4 + 3 * bm * d_pool * 4 + pool_w.size * 6)
    return pl.pallas_call(
        kern,
        grid=(m // bm,),
        in_specs=[pl.BlockSpec((bm, d), lambda i: (i, 0)),
                  _const_spec((d, d_pool), lambda i: (0, 0)),
                  _const_spec(pool_w.shape, lambda i: (0, 0, 0)),
                  _const_spec((1, d_pool), lambda i: (0, 0))],
        out_specs=pl.BlockSpec((bm, d_pool), lambda i: (i, 0)),
        out_shape=jax.ShapeDtypeStruct((m, d_pool), BF16),
        scratch_shapes=[pltpu.VMEM((d, d_pool), BF16),
                        pltpu.VMEM(pool_w.shape, BF16),
                        pltpu.VMEM((halo + bm, d_pool), F32)],
        compiler_params=_params(("arbitrary",), vmem),
        name="pool_branch",
    )(h, w_in, pool_w, pool_scale.reshape(1, d_pool))


def _cast_slab_specs(rows, d, ni):
    index = lambda j, i: (j * ni + i, 0)
    return pl.BlockSpec((rows, d), index), pl.BlockSpec((rows, d), index)


def _glu_conv_kernel(h_ref, wv_ref, wg_ref, cw_ref, cb_ref, wo_ref, o_ref, wobf_ref,
                     wvbf, wgbf, gbuf, phase, *, bm, bn, seq, halo, taps, sub, rows, lanes):
    i = pl.program_id(1)

    @pl.when(i == 0)
    def _():
        wvbf[...] = wv_ref[...].astype(BF16)
        wgbf[...] = wg_ref[...].astype(BF16)

    wobf_ref[...] = wo_ref[...].astype(BF16)
    _load_history(gbuf, i, bm=bm, seq=seq, halo=halo)

    max_a = (taps - 1) // V7X_SUBLANES
    lead = (max_a + 1) * V7X_SUBLANES
    for s0 in range(0, bm, sub):
        h = h_ref[s0:s0 + sub, :]
        v = jnp.dot(h, wvbf[...], preferred_element_type=F32)
        gate = jnp.dot(h, wgbf[...], preferred_element_type=F32)
        gbuf[halo + s0:halo + s0 + sub, :] = v * jax.nn.sigmoid(gate)
        for c0 in range(0, bn, lanes):
            cs = slice(c0, c0 + lanes)
            win = gbuf[halo + s0 - lead:halo + s0 + sub, cs]
            for r in range(1, V7X_SUBLANES):
                phase[r - 1, :, cs] = pltpu.roll(win, r, 0)
        for r0 in range(0, sub, rows):
            for c0 in range(0, bn, lanes):
                cs = slice(c0, c0 + lanes)
                acc = jnp.broadcast_to(cb_ref[:, cs], (rows, lanes))
                for s in range(taps):
                    a, r = divmod(s, V7X_SUBLANES)
                    off = r0 + lead - V7X_SUBLANES * a
                    if r == 0:
                        src = gbuf[halo + s0 - lead + off:halo + s0 - lead + off + rows, cs]
                    else:
                        src = phase[r - 1, off:off + rows, cs]
                    acc = acc + cw_ref[taps - 1 - s:taps - s, cs] * src
                o_ref[s0 + r0:s0 + r0 + rows, cs] = acc


def _glu_conv_branch(h, w_in, conv_w, conv_b, w_out, *, d_pool, bm, bn, seq):
    m, d = h.shape
    taps, d_conv = conv_w.shape
    halo = _halo_rows(taps - 1)
    nj, ni = d_conv // bn, m // bm
    v_blk0 = d_pool // bn
    g_blk0 = (d_pool + d_conv) // bn
    ko, do = w_out.shape
    slab = ko // (nj * ni)
    wo_in, wo_out = _cast_slab_specs(slab, do, ni)
    sub = 256
    lead = _halo_rows(taps)
    assert halo >= lead and bm % sub == 0
    kern = functools.partial(_glu_conv_kernel, bm=bm, bn=bn, seq=seq, halo=halo, taps=taps,
                             sub=sub, rows=128, lanes=V7X_LANES)
    phase_shape = (V7X_SUBLANES - 1, lead + sub, bn)
    vmem = (2 * 2 * d * bn * 4 + 2 * d * bn * 2 + 2 * bm * d * 2 + 2 * bm * bn * 4
            + (halo + bm) * bn * 4 + 4 * sub * bn * 4 + 2 * slab * do * 6
            + 4 * phase_shape[0] * phase_shape[1] * bn)
    return pl.pallas_call(
        kern,
        grid=(nj, ni),
        in_specs=[pl.BlockSpec((bm, d), lambda j, i: (i, 0)),
                  pl.BlockSpec((d, bn), lambda j, i: (0, v_blk0 + j)),
                  pl.BlockSpec((d, bn), lambda j, i: (0, g_blk0 + j)),
                  pl.BlockSpec((taps, bn), lambda j, i: (0, j)),
                  pl.BlockSpec((1, bn), lambda j, i: (0, j)),
                  wo_in],
        out_specs=[pl.BlockSpec((bm, bn), lambda j, i: (i, j)), wo_out],
        out_shape=[jax.ShapeDtypeStruct((m, d_conv), F32),
                   jax.ShapeDtypeStruct((ko, do), BF16)],
        scratch_shapes=[pltpu.VMEM((d, bn), BF16),
                        pltpu.VMEM((d, bn), BF16),
                        pltpu.VMEM((halo + bm, bn), F32),
                        pltpu.VMEM(phase_shape, F32)],
        compiler_params=_params(("arbitrary", "arbitrary"), vmem),
        name="glu_conv_branch",
    )(h, w_in, w_in, conv_w, conv_b.reshape(1, d_conv), w_out)


def _residual_norms(m, x_ref, gpost_ref, gnext_ref, xo_ref, ho_ref):
    x_new = x_ref[...] + _rms(m, gpost_ref[...])
    xo_ref[...] = x_new
    if ho_ref is not None:
        ho_ref[...] = _rms(x_new, gnext_ref[...]).astype(ho_ref.dtype)


def _out_proj_l0_kernel(yp_ref, c_ref, x_ref, w_ref, lng_ref, lnb_ref, gpost_ref, gnext_ref,
                        xo_ref, ho_ref, *, d_pool):
    c = c_ref[...]
    mu = jnp.mean(c, axis=-1, keepdims=True)
    cc = c - mu
    var = jnp.mean(cc * cc, axis=-1, keepdims=True)
    ln = cc * lax.rsqrt(var + EPS) * lng_ref[...] + lnb_ref[...]
    y_conv = (ln * jax.nn.sigmoid(ln)).astype(BF16)
    m = jnp.dot(yp_ref[...], w_ref[0:d_pool, :], preferred_element_type=F32)
    m = m + jnp.dot(y_conv, w_ref[d_pool:, :], preferred_element_type=F32)
    _residual_norms(m, x_ref, gpost_ref, gnext_ref, xo_ref, ho_ref)


def _out_proj_l1_kernel(y_ref, x_ref, w_ref, gpost_ref, gnext_ref, xo_ref, ho_ref):
    m = jnp.dot(y_ref[...], w_ref[...], preferred_element_type=F32)
    _residual_norms(m, x_ref, gpost_ref, gnext_ref, xo_ref, ho_ref)


def _out_proj(ys, x, w_bf, g_post, g_next, ln=None, *, bm):
    m, d = x.shape
    k = w_bf.shape[0]
    row = lambda i: (i, 0)
    fixed = lambda i: (0, 0)
    vec = lambda a: a.reshape(1, -1)
    if ln is None:
        (y,) = ys
        kern = _out_proj_l1_kernel
        ins = [y, x, w_bf, vec(g_post), vec(g_next)]
        in_specs = [pl.BlockSpec((bm, k), row), pl.BlockSpec((bm, d), row),
                    _const_spec((k, d), fixed), _const_spec((1, d), fixed), _const_spec((1, d), fixed)]
        tile_bytes = bm * k * 2
    else:
        y_pool, c = ys
        d_pool, d_conv = y_pool.shape[1], c.shape[1]
        kern = functools.partial(_out_proj_l0_kernel, d_pool=d_pool)
        ins = [y_pool, c, x, w_bf, vec(ln[0]), vec(ln[1]), vec(g_post), vec(g_next)]
        in_specs = [pl.BlockSpec((bm, d_pool), row), pl.BlockSpec((bm, d_conv), row),
                    pl.BlockSpec((bm, d), row), _const_spec((k, d), fixed),
                    _const_spec((1, d_conv), fixed), _const_spec((1, d_conv), fixed),
                    _const_spec((1, d), fixed), _const_spec((1, d), fixed)]
        tile_bytes = bm * d_pool * 2 + bm * d_conv * 4
    vmem = k * d * 2 + 2 * tile_bytes + 2 * bm * d * (4 + 4 + 2) + 5 * bm * d * 4
    return pl.pallas_call(
        kern,
        grid=(m // bm,),
        in_specs=in_specs,
        out_specs=[pl.BlockSpec((bm, d), row), pl.BlockSpec((bm, d), row)],
        out_shape=[jax.ShapeDtypeStruct((m, d), F32), jax.ShapeDtypeStruct((m, d), BF16)],
        compiler_params=_params(("arbitrary",), vmem),
        name="out_proj_l0" if ln is not None else "out_proj_l1",
    )(*ins)


def _ffn_up_kernel(h_ref, w_ref, w2_ref, o_ref, w2bf_ref, wbf):
    @pl.when(pl.program_id(1) == 0)
    def _():
        wbf[...] = w_ref[...].astype(BF16)

    w2bf_ref[...] = w2_ref[...].astype(BF16)
    a = jnp.maximum(jnp.dot(h_ref[...], wbf[...], preferred_element_type=F32), 0.0)
    o_ref[...] = (a * a).astype(o_ref.dtype)


def _ffn_up(h, w1, w2, layer, *, bm, bn):
    m, d = h.shape
    f = w1.shape[2]
    nj, ni = f // bn, m // bm
    slab = f // (nj * ni)
    slab_index = lambda j, i: (j * ni + i, 0)
    vmem = (2 * d * bn * 4 + d * bn * 2 + 2 * bm * d * 2 + 2 * bm * bn * 2 + 3 * bm * bn * 4
            + 2 * slab * d * 6)
    return pl.pallas_call(
        _ffn_up_kernel,
        grid=(nj, ni),
        in_specs=[pl.BlockSpec((bm, d), lambda j, i: (i, 0)),
                  pl.BlockSpec((None, d, bn), lambda j, i: (layer, 0, j)),
                  pl.BlockSpec((None, slab, d), lambda j, i: (layer, j * ni + i, 0))],
        out_specs=[pl.BlockSpec((bm, bn), lambda j, i: (i, j)),
                   pl.BlockSpec((slab, d), slab_index)],
        out_shape=[jax.ShapeDtypeStruct((m, f), BF16),
                   jax.ShapeDtypeStruct((f, d), BF16)],
        scratch_shapes=[pltpu.VMEM((d, bn), BF16)],
        compiler_params=_params(("arbitrary", "arbitrary"), vmem),
        name="ffn_up",
    )(h, w1, w2)


def _ffn_down_kernel(a_ref, w_ref, x_ref, gpost_ref, gnext_ref, xo_ref, *rest, with_next):
    ho_ref = rest[0] if with_next else None
    m = jnp.dot(a_ref[...], w_ref[...], preferred_element_type=F32)
    _residual_norms(m, x_ref, gpost_ref, gnext_ref, xo_ref, ho_ref)


def _ffn_down(a, w2_bf, x, g_post, g_next, *, bm):
    m, f = a.shape
    d = w2_bf.shape[1]
    with_next = g_next is not None
    if not with_next:
        g_next = g_post
    row = lambda i: (i, 0)
    fixed = lambda i: (0, 0)
    out_specs = [pl.BlockSpec((bm, d), row)]
    out_shape = [jax.ShapeDtypeStruct((m, d), F32)]
    if with_next:
        out_specs.append(pl.BlockSpec((bm, d), row))
        out_shape.append(jax.ShapeDtypeStruct((m, d), BF16))
    vmem = f * d * 2 + 2 * bm * f * 2 + 2 * bm * d * (4 + 4 + 2) + 3 * bm * d * 4
    outs = pl.pallas_call(
        functools.partial(_ffn_down_kernel, with_next=with_next),
        grid=(m // bm,),
        in_specs=[pl.BlockSpec((bm, f), row),
                  _const_spec((f, d), fixed),
                  pl.BlockSpec((bm, d), row),
                  _const_spec((1, d), fixed), _const_spec((1, d), fixed)],
        out_specs=out_specs,
        out_shape=out_shape,
        compiler_params=_params(("arbitrary",), vmem),
        name="ffn_down",
    )(a, w2_bf, x, g_post.reshape(1, d), g_next.reshape(1, d))
    return outs if with_next else (outs[0], None)


def _gated_conv_kernel(h_ref, wb_ref, wc_ref, wu_ref, cw_ref, wo_ref, o_ref, wobf_ref,
                       wbbf, wcbf, wubf, pbuf, *, bm, seq, halo, taps):
    i = pl.program_id(1)

    @pl.when(i == 0)
    def _():
        wbbf[...] = wb_ref[...].astype(BF16)
        wcbf[...] = wc_ref[...].astype(BF16)
        wubf[...] = wu_ref[...].astype(BF16)

    wobf_ref[...] = wo_ref[...].astype(BF16)
    _load_history(pbuf, i, bm=bm, seq=seq, halo=halo)
    h = h_ref[...]
    zc = jnp.dot(h, wcbf[...], preferred_element_type=F32)
    zu = jnp.dot(h, wubf[...], preferred_element_type=F32)
    pbuf[halo:, :] = zc * zu
    cw = cw_ref[...]
    conv = cw[taps - 1:taps, :] * pbuf[halo:halo + bm, :]
    for k in range(taps - 1):
        start = halo - (taps - 1) + k
        conv = conv + cw[k:k + 1, :] * pbuf[start:start + bm, :]
    zb = jnp.dot(h, wbbf[...], preferred_element_type=F32)
    o_ref[...] = (zb * conv).astype(o_ref.dtype)


def _gated_conv_branch(h, w_in, conv_w, w_out, *, bm, bn, seq):
    m, d = h.shape
    taps, d_short = conv_w.shape
    halo = _halo_rows(taps - 1)
    nj, ni = d_short // bn, m // bm
    ko, do = w_out.shape
    slab = ko // (nj * ni)
    wo_in, wo_out = _cast_slab_specs(slab, do, ni)
    kern = functools.partial(_gated_conv_kernel, bm=bm, seq=seq, halo=halo, taps=taps)
    vmem = (3 * 2 * d * bn * 4 + 3 * d * bn * 2 + 2 * bm * d * 2 + 2 * bm * bn * 2
            + (halo + bm) * bn * 4 + 5 * bm * bn * 4 + 2 * slab * do * 6)
    return pl.pallas_call(
        kern,
        grid=(nj, ni),
        in_specs=[pl.BlockSpec((bm, d), lambda j, i: (i, 0)),
                  pl.BlockSpec((d, bn), lambda j, i: (0, j)),
                  pl.BlockSpec((d, bn), lambda j, i: (0, nj + j)),
                  pl.BlockSpec((d, bn), lambda j, i: (0, 2 * nj + j)),
                  pl.BlockSpec((taps, bn), lambda j, i: (0, j)),
                  wo_in],
        out_specs=[pl.BlockSpec((bm, bn), lambda j, i: (i, j)), wo_out],
        out_shape=[jax.ShapeDtypeStruct((m, d_short), BF16),
                   jax.ShapeDtypeStruct((ko, do), BF16)],
        scratch_shapes=[pltpu.VMEM((d, bn), BF16), pltpu.VMEM((d, bn), BF16),
                        pltpu.VMEM((d, bn), BF16), pltpu.VMEM((halo + bm, bn), F32)],
        compiler_params=_params(("arbitrary", "arbitrary"), vmem),
        name="gated_conv_branch",
    )(h, w_in, w_in, w_in, conv_w, w_out)


def kernel(x, mix_pre_g, mix_post_g, ffn_pre_g, ffn_post_g, ab_w_in, pool_w, pool_scale,
           conv_w, conv_b, conv_ln_g, conv_ln_b, ab_w_out, sc_w_in, sc_conv_w, sc_w_out,
           ffn_w1, ffn_w2):
    batch, seq, d = x.shape
    depth = mix_pre_g.shape[0]
    d_pool = pool_scale.shape[1]
    xs = x.reshape(batch * seq, d)
    h = _rms_cast(xs, mix_pre_g[0], bm=512)
    for layer in range(depth):
        i = layer // 2
        if layer % 2 == 0:
            y_pool = _pool_branch(h, ab_w_in[i], pool_w[i], pool_scale[i], bm=512, seq=seq)
            c, w_out_bf = _glu_conv_branch(h, ab_w_in[i], conv_w[i], conv_b[i], ab_w_out[i],
                                           d_pool=d_pool, bm=1024, bn=512, seq=seq)
            xs, h = _out_proj((y_pool, c), xs, w_out_bf, mix_post_g[layer], ffn_pre_g[layer],
                              ln=(conv_ln_g[i], conv_ln_b[i]), bm=512)
        else:
            y, w_out_bf = _gated_conv_branch(h, sc_w_in[i], sc_conv_w[i], sc_w_out[i],
                                             bm=1024, bn=512, seq=seq)
            xs, h = _out_proj((y,), xs, w_out_bf, mix_post_g[layer], ffn_pre_g[layer], bm=512)
        a, w2_bf = _ffn_up(h, ffn_w1, ffn_w2, layer, bm=1024, bn=1024)
        g_next = mix_pre_g[layer + 1] if layer + 1 < depth else None
        xs, h = _ffn_down(a, w2_bf, xs, ffn_post_g[layer], g_next, bm=256)
    return xs.reshape(batch, seq, d)
```

```python
import functools

import jax
import jax.numpy as jnp
from jax import lax
from jax.experimental import pallas as pl
from jax.experimental.pallas import tpu as pltpu

EPS = 1e-6
POOL_WINDOWS = (2, 4, 8, 16)

V7X_SUBLANES = 8
V7X_LANES = 128
V7X_VMEM_LIMIT_CAP = 60000 * 1024

F32 = jnp.float32
BF16 = jnp.bfloat16


def _halo_rows(n):
    return -(-n // V7X_SUBLANES) * V7X_SUBLANES


def _params(semantics, vmem_bytes):
    return pltpu.CompilerParams(
        dimension_semantics=semantics,
        vmem_limit_bytes=min(int(vmem_bytes), V7X_VMEM_LIMIT_CAP))


def _const_spec(shape, index_map):
    return pl.BlockSpec(shape, index_map, pipeline_mode=pl.Buffered(1))


def _cast_weight(dst, src, *, rows=256):
    n = src.shape[0] // rows

    def body(c, carry):
        r = pl.multiple_of(c * rows, rows)
        dst[pl.ds(r, rows), :] = src[pl.ds(r, rows), :].astype(BF16)
        return carry

    lax.fori_loop(0, n, body, 0)


def _rms(x, g):
    ms = jnp.mean(x * x, axis=-1, keepdims=True)
    return x * lax.rsqrt(ms + EPS) * g


def _load_history(buf, i, *, bm, seq, halo):
    first = lax.rem(i * bm, seq) == 0

    @pl.when(first)
    def _():
        buf[0:halo, :] = jnp.zeros((halo, buf.shape[1]), buf.dtype)

    @pl.when(jnp.logical_not(first))
    def _():
        buf[0:halo, :] = buf[bm:bm + halo, :]


def _pool_kernel(x_ref, g_ref, w_ref, pw_ref, ps_ref, o_ref, h_ref, wbf, pwbf, ubuf,
                 *, bm, seq, halo, group):
    i = pl.program_id(0)

    @pl.when(i == 0)
    def _():
        _cast_weight(wbf, w_ref)
        pwbf[...] = pw_ref[...].astype(BF16)

    h = _rms(x_ref[...], g_ref[...]).astype(BF16)
    h_ref[...] = h
    _load_history(ubuf, i, bm=bm, seq=seq, halo=halo)
    ubuf[halo:, :] = jnp.dot(h, wbf[...], preferred_element_type=F32)

    pos = lax.rem(i * bm, seq) + lax.broadcasted_iota(jnp.int32, (bm, 1), 0)
    for g, w in enumerate(POOL_WINDOWS):
        cols = slice(g * group, (g + 1) * group)
        s = ubuf[:, cols]
        span = 1
        while span < w:
            s = s + pltpu.roll(s, span, 0)
            span *= 2
        u = ubuf[halo:halo + bm, cols]
        cnt = jnp.minimum(pos + 1, w).astype(F32)
        pooled = s[halo:, :] / cnt - u
        mixed = jnp.dot(pooled.astype(BF16), pwbf[g], preferred_element_type=F32)
        o_ref[:, cols] = (mixed * ps_ref[:, cols]).astype(o_ref.dtype)


def _pool_branch(x, g_pre, w_in, pool_w, pool_scale, *, bm, seq):
    m, d = x.shape
    n_groups, group, _ = pool_w.shape
    d_pool = n_groups * group
    halo = _halo_rows(max(POOL_WINDOWS) - 1)
    assert all(w & (w - 1) == 0 for w in POOL_WINDOWS)
    kern = functools.partial(_pool_kernel, bm=bm, seq=seq, halo=halo, group=group)
    vmem = (d * d_pool * (4 + 2) + 2 * bm * d * (4 + 2) + 2 * bm * d_pool * 2
            + (halo + bm) * d_pool * 4 + 3 * bm * d_pool * 4 + pool_w.size * 6 + 3 * bm * d * 4)
    return pl.pallas_call(
        kern,
        grid=(m // bm,),
        in_specs=[pl.BlockSpec((bm, d), lambda i: (i, 0)),
                  _const_spec((1, d), lambda i: (0, 0)),
                  _const_spec((d, d_pool), lambda i: (0, 0)),
                  _const_spec(pool_w.shape, lambda i: (0, 0, 0)),
                  _const_spec((1, d_pool), lambda i: (0, 0))],
        out_specs=[pl.BlockSpec((bm, d_pool), lambda i: (i, 0)),
                   pl.BlockSpec((bm, d), lambda i: (i, 0))],
        out_shape=[jax.ShapeDtypeStruct((m, d_pool), BF16),
                   jax.ShapeDtypeStruct((m, d), BF16)],
        scratch_shapes=[pltpu.VMEM((d, d_pool), BF16),
                        pltpu.VMEM(pool_w.shape, BF16),
                        pltpu.VMEM((halo + bm, d_pool), F32)],
        compiler_params=_params(("arbitrary",), vmem),
        name="pool_branch",
    )(x, g_pre.reshape(1, d), w_in, pool_w, pool_scale.reshape(1, d_pool))


def _cast_slab_specs(rows, d, ni):
    index = lambda j, i: (j * ni + i, 0)
    return pl.BlockSpec((rows, d), index), pl.BlockSpec((rows, d), index)


def _glu_conv_kernel(h_ref, wv_ref, wg_ref, cw_ref, cb_ref, wo_ref, o_ref, wobf_ref,
                     wvbf, wgbf, gbuf, phase, *, bm, bn, seq, halo, taps, sub, rows, lanes):
    i = pl.program_id(1)

    @pl.when(i == 0)
    def _():
        _cast_weight(wvbf, wv_ref)
        _cast_weight(wgbf, wg_ref)

    wobf_ref[...] = wo_ref[...].astype(BF16)
    _load_history(gbuf, i, bm=bm, seq=seq, halo=halo)

    max_a = (taps - 1) // V7X_SUBLANES
    lead = (max_a + 1) * V7X_SUBLANES
    for s0 in range(0, bm, sub):
        h = h_ref[s0:s0 + sub, :]
        v = jnp.dot(h, wvbf[...], preferred_element_type=F32)
        gate = jnp.dot(h, wgbf[...], preferred_element_type=F32)
        gbuf[halo + s0:halo + s0 + sub, :] = v * jax.nn.sigmoid(gate)
        for c0 in range(0, bn, lanes):
            cs = slice(c0, c0 + lanes)
            win = gbuf[halo + s0 - lead:halo + s0 + sub, cs]
            for r in range(1, V7X_SUBLANES):
                phase[r - 1, :, cs] = pltpu.roll(win, r, 0)
        for r0 in range(0, sub, rows):
            for c0 in range(0, bn, lanes):
                cs = slice(c0, c0 + lanes)
                acc = jnp.broadcast_to(cb_ref[:, cs], (rows, lanes))
                for s in range(taps):
                    a, r = divmod(s, V7X_SUBLANES)
                    off = r0 + lead - V7X_SUBLANES * a
                    if r == 0:
                        src = gbuf[halo + s0 - lead + off:halo + s0 - lead + off + rows, cs]
                    else:
                        src = phase[r - 1, off:off + rows, cs]
                    acc = acc + cw_ref[taps - 1 - s:taps - s, cs] * src
                o_ref[s0 + r0:s0 + r0 + rows, cs] = acc


def _glu_conv_branch(h, w_in, conv_w, conv_b, w_out, *, d_pool, bm, bn, seq):
    m, d = h.shape
    taps, d_conv = conv_w.shape
    halo = _halo_rows(taps - 1)
    nj, ni = d_conv // bn, m // bm
    v_blk0 = d_pool // bn
    g_blk0 = (d_pool + d_conv) // bn
    ko, do = w_out.shape
    slab = ko // (nj * ni)
    wo_in, wo_out = _cast_slab_specs(slab, do, ni)
    sub = 256
    lead = _halo_rows(taps)
    assert halo >= lead and bm % sub == 0
    kern = functools.partial(_glu_conv_kernel, bm=bm, bn=bn, seq=seq, halo=halo, taps=taps,
                             sub=sub, rows=128, lanes=V7X_LANES)
    phase_shape = (V7X_SUBLANES - 1, lead + sub, bn)
    vmem = (2 * 2 * d * bn * 4 + 2 * d * bn * 2 + 2 * bm * d * 2 + 2 * bm * bn * 4
            + (halo + bm) * bn * 4 + 4 * sub * bn * 4 + 2 * slab * do * 6
            + 4 * phase_shape[0] * phase_shape[1] * bn)
    return pl.pallas_call(
        kern,
        grid=(nj, ni),
        in_specs=[pl.BlockSpec((bm, d), lambda j, i: (i, 0)),
                  pl.BlockSpec((d, bn), lambda j, i: (0, v_blk0 + j)),
                  pl.BlockSpec((d, bn), lambda j, i: (0, g_blk0 + j)),
                  pl.BlockSpec((taps, bn), lambda j, i: (0, j)),
                  pl.BlockSpec((1, bn), lambda j, i: (0, j)),
                  wo_in],
        out_specs=[pl.BlockSpec((bm, bn), lambda j, i: (i, j)), wo_out],
        out_shape=[jax.ShapeDtypeStruct((m, d_conv), F32),
                   jax.ShapeDtypeStruct((ko, do), BF16)],
        scratch_shapes=[pltpu.VMEM((d, bn), BF16),
                        pltpu.VMEM((d, bn), BF16),
                        pltpu.VMEM((halo + bm, bn), F32),
                        pltpu.VMEM(phase_shape, F32)],
        compiler_params=_params(("arbitrary", "arbitrary"), vmem),
        name="glu_conv_branch",
    )(h, w_in, w_in, conv_w, conv_b.reshape(1, d_conv), w_out)


def _residual_norms(m, x_ref, gpost_ref, gnext_ref, xo_ref, ho_ref, rows=slice(None)):
    x_new = x_ref[rows, :] + _rms(m, gpost_ref[...])
    xo_ref[rows, :] = x_new
    if ho_ref is not None:
        ho_ref[rows, :] = _rms(x_new, gnext_ref[...]).astype(ho_ref.dtype)


def _out_proj_l0_kernel(yp_ref, c_ref, x_ref, w_ref, lng_ref, lnb_ref, gpost_ref, gnext_ref,
                        xo_ref, ho_ref, *, d_pool, bm, sub):
    for s0 in range(0, bm, sub):
        rows = slice(s0, s0 + sub)
        c = c_ref[rows, :]
        mu = jnp.mean(c, axis=-1, keepdims=True)
        cc = c - mu
        var = jnp.mean(cc * cc, axis=-1, keepdims=True)
        ln = cc * lax.rsqrt(var + EPS) * lng_ref[...] + lnb_ref[...]
        y_conv = (ln * jax.nn.sigmoid(ln)).astype(BF16)
        m = jnp.dot(yp_ref[rows, :], w_ref[0:d_pool, :], preferred_element_type=F32)
        m = m + jnp.dot(y_conv, w_ref[d_pool:, :], preferred_element_type=F32)
        _residual_norms(m, x_ref, gpost_ref, gnext_ref, xo_ref, ho_ref, rows)


def _out_proj_l1_kernel(y_ref, x_ref, w_ref, gpost_ref, gnext_ref, xo_ref, ho_ref, *, bm, sub):
    for s0 in range(0, bm, sub):
        rows = slice(s0, s0 + sub)
        m = jnp.dot(y_ref[rows, :], w_ref[...], preferred_element_type=F32)
        _residual_norms(m, x_ref, gpost_ref, gnext_ref, xo_ref, ho_ref, rows)


def _out_proj(ys, x, w_bf, g_post, g_next, ln=None, *, bm, sub=256):
    assert bm % sub == 0
    m, d = x.shape
    k = w_bf.shape[0]
    row = lambda i: (i, 0)
    fixed = lambda i: (0, 0)
    vec = lambda a: a.reshape(1, -1)
    if ln is None:
        (y,) = ys
        kern = functools.partial(_out_proj_l1_kernel, bm=bm, sub=sub)
        ins = [y, x, w_bf, vec(g_post), vec(g_next)]
        in_specs = [pl.BlockSpec((bm, k), row), pl.BlockSpec((bm, d), row),
                    _const_spec((k, d), fixed), _const_spec((1, d), fixed), _const_spec((1, d), fixed)]
        tile_bytes = bm * k * 2
    else:
        y_pool, c = ys
        d_pool, d_conv = y_pool.shape[1], c.shape[1]
        kern = functools.partial(_out_proj_l0_kernel, d_pool=d_pool, bm=bm, sub=sub)
        ins = [y_pool, c, x, w_bf, vec(ln[0]), vec(ln[1]), vec(g_post), vec(g_next)]
        in_specs = [pl.BlockSpec((bm, d_pool), row), pl.BlockSpec((bm, d_conv), row),
                    pl.BlockSpec((bm, d), row), _const_spec((k, d), fixed),
                    _const_spec((1, d_conv), fixed), _const_spec((1, d_conv), fixed),
                    _const_spec((1, d), fixed), _const_spec((1, d), fixed)]
        tile_bytes = bm * d_pool * 2 + bm * d_conv * 4
    vmem = k * d * 2 + 2 * tile_bytes + 2 * bm * d * (4 + 4 + 2) + 5 * bm * d * 4
    return pl.pallas_call(
        kern,
        grid=(m // bm,),
        in_specs=in_specs,
        out_specs=[pl.BlockSpec((bm, d), row), pl.BlockSpec((bm, d), row)],
        out_shape=[jax.ShapeDtypeStruct((m, d), F32), jax.ShapeDtypeStruct((m, d), BF16)],
        compiler_params=_params(("arbitrary",), vmem),
        name="out_proj_l0" if ln is not None else "out_proj_l1",
    )(*ins)


def _ffn_up_kernel(h_ref, w_ref, w2_ref, o_ref, w2bf_ref, wbf, *, bm, sub):
    @pl.when(pl.program_id(1) == 0)
    def _():
        _cast_weight(wbf, w_ref)

    w2bf_ref[...] = w2_ref[...].astype(BF16)
    for s0 in range(0, bm, sub):
        rows = slice(s0, s0 + sub)
        a = jnp.maximum(jnp.dot(h_ref[rows, :], wbf[...], preferred_element_type=F32), 0.0)
        o_ref[rows, :] = (a * a).astype(o_ref.dtype)


def _ffn_up(h, w1, w2, layer, *, bm, bn):
    m, d = h.shape
    f = w1.shape[2]
    nj, ni = f // bn, m // bm
    slab = f // (nj * ni)
    slab_index = lambda j, i: (j * ni + i, 0)
    sub = min(bm, 512)
    vmem = (2 * d * bn * 4 + d * bn * 2 + 2 * bm * d * 2 + 2 * bm * bn * 2 + 3 * sub * bn * 4
            + 2 * slab * d * 6)
    return pl.pallas_call(
        functools.partial(_ffn_up_kernel, bm=bm, sub=sub),
        grid=(nj, ni),
        in_specs=[pl.BlockSpec((bm, d), lambda j, i: (i, 0)),
                  pl.BlockSpec((None, d, bn), lambda j, i: (layer, 0, j)),
                  pl.BlockSpec((None, slab, d), lambda j, i: (layer, j * ni + i, 0))],
        out_specs=[pl.BlockSpec((bm, bn), lambda j, i: (i, j)),
                   pl.BlockSpec((slab, d), slab_index)],
        out_shape=[jax.ShapeDtypeStruct((m, f), BF16),
                   jax.ShapeDtypeStruct((f, d), BF16)],
        scratch_shapes=[pltpu.VMEM((d, bn), BF16)],
        compiler_params=_params(("arbitrary", "arbitrary"), vmem),
        name="ffn_up",
    )(h, w1, w2)


def _ffn_down_kernel(a_ref, w_ref, x_ref, gpost_ref, gnext_ref, xo_ref, *rest, with_next):
    ho_ref = rest[0] if with_next else None
    m = jnp.dot(a_ref[...], w_ref[...], preferred_element_type=F32)
    _residual_norms(m, x_ref, gpost_ref, gnext_ref, xo_ref, ho_ref)


def _ffn_down(a, w2_bf, x, g_post, g_next, *, bm):
    m, f = a.shape
    d = w2_bf.shape[1]
    with_next = g_next is not None
    if not with_next:
        g_next = g_post
    row = lambda i: (i, 0)
    fixed = lambda i: (0, 0)
    out_specs = [pl.BlockSpec((bm, d), row)]
    out_shape = [jax.ShapeDtypeStruct((m, d), F32)]
    if with_next:
        out_specs.append(pl.BlockSpec((bm, d), row))
        out_shape.append(jax.ShapeDtypeStruct((m, d), BF16))
    vmem = f * d * 2 + 2 * bm * f * 2 + 2 * bm * d * (4 + 4 + 2) + 3 * bm * d * 4
    outs = pl.pallas_call(
        functools.partial(_ffn_down_kernel, with_next=with_next),
        grid=(m // bm,),
        in_specs=[pl.BlockSpec((bm, f), row),
                  _const_spec((f, d), fixed),
                  pl.BlockSpec((bm, d), row),
                  _const_spec((1, d), fixed), _const_spec((1, d), fixed)],
        out_specs=out_specs,
        out_shape=out_shape,
        compiler_params=_params(("arbitrary",), vmem),
        name="ffn_down",
    )(a, w2_bf, x, g_post.reshape(1, d), g_next.reshape(1, d))
    return outs if with_next else (outs[0], None)


def _gated_conv_kernel(h_ref, wb_ref, wc_ref, wu_ref, cw_ref, wo_ref, o_ref, wobf_ref,
                       wbbf, wcbf, wubf, pbuf, *, bm, seq, halo, taps):
    i = pl.program_id(1)

    @pl.when(i == 0)
    def _():
        _cast_weight(wbbf, wb_ref)
        _cast_weight(wcbf, wc_ref)
        _cast_weight(wubf, wu_ref)

    wobf_ref[...] = wo_ref[...].astype(BF16)
    _load_history(pbuf, i, bm=bm, seq=seq, halo=halo)
    h = h_ref[...]
    zc = jnp.dot(h, wcbf[...], preferred_element_type=F32)
    zu = jnp.dot(h, wubf[...], preferred_element_type=F32)
    pbuf[halo:, :] = zc * zu
    cw = cw_ref[...]
    conv = cw[taps - 1:taps, :] * pbuf[halo:halo + bm, :]
    for k in range(taps - 1):
        start = halo - (taps - 1) + k
        conv = conv + cw[k:k + 1, :] * pbuf[start:start + bm, :]
    zb = jnp.dot(h, wbbf[...], preferred_element_type=F32)
    o_ref[...] = (zb * conv).astype(o_ref.dtype)


def _gated_conv_branch(h, w_in, conv_w, w_out, *, bm, bn, seq):
    m, d = h.shape
    taps, d_short = conv_w.shape
    halo = _halo_rows(taps - 1)
    nj, ni = d_short // bn, m // bm
    ko, do = w_out.shape
    slab = ko // (nj * ni)
    wo_in, wo_out = _cast_slab_specs(slab, do, ni)
    kern = functools.partial(_gated_conv_kernel, bm=bm, seq=seq, halo=halo, taps=taps)
    vmem = (3 * 2 * d * bn * 4 + 3 * d * bn * 2 + 2 * bm * d * 2 + 2 * bm * bn * 2
            + (halo + bm) * bn * 4 + 5 * bm * bn * 4 + 2 * slab * do * 6)
    return pl.pallas_call(
        kern,
        grid=(nj, ni),
        in_specs=[pl.BlockSpec((bm, d), lambda j, i: (i, 0)),
                  pl.BlockSpec((d, bn), lambda j, i: (0, j)),
                  pl.BlockSpec((d, bn), lambda j, i: (0, nj + j)),
                  pl.BlockSpec((d, bn), lambda j, i: (0, 2 * nj + j)),
                  pl.BlockSpec((taps, bn), lambda j, i: (0, j)),
                  wo_in],
        out_specs=[pl.BlockSpec((bm, bn), lambda j, i: (i, j)), wo_out],
        out_shape=[jax.ShapeDtypeStruct((m, d_short), BF16),
                   jax.ShapeDtypeStruct((ko, do), BF16)],
        scratch_shapes=[pltpu.VMEM((d, bn), BF16), pltpu.VMEM((d, bn), BF16),
                        pltpu.VMEM((d, bn), BF16), pltpu.VMEM((halo + bm, bn), F32)],
        compiler_params=_params(("arbitrary", "arbitrary"), vmem),
        name="gated_conv_branch",
    )(h, w_in, w_in, w_in, conv_w, w_out)


def kernel(x, mix_pre_g, mix_post_g, ffn_pre_g, ffn_post_g, ab_w_in, pool_w, pool_scale,
           conv_w, conv_b, conv_ln_g, conv_ln_b, ab_w_out, sc_w_in, sc_conv_w, sc_w_out,
           ffn_w1, ffn_w2):
    batch, seq, d = x.shape
    depth = mix_pre_g.shape[0]
    d_pool = pool_scale.shape[1]
    xs = x.reshape(batch * seq, d)
    h = None
    for layer in range(depth):
        i = layer // 2
        if layer % 2 == 0:
            y_pool, h = _pool_branch(xs, mix_pre_g[layer], ab_w_in[i], pool_w[i], pool_scale[i],
                                     bm=512, seq=seq)
            c, w_out_bf = _glu_conv_branch(h, ab_w_in[i], conv_w[i], conv_b[i], ab_w_out[i],
                                           d_pool=d_pool, bm=1024, bn=512, seq=seq)
            xs, h = _out_proj((y_pool, c), xs, w_out_bf, mix_post_g[layer], ffn_pre_g[layer],
                              ln=(conv_ln_g[i], conv_ln_b[i]), bm=512)
        else:
            y, w_out_bf = _gated_conv_branch(h, sc_w_in[i], sc_conv_w[i], sc_w_out[i],
                                             bm=1024, bn=512, seq=seq)
            xs, h = _out_proj((y,), xs, w_out_bf, mix_post_g[layer], ffn_pre_g[layer], bm=512)
        a, w2_bf = _ffn_up(h, ffn_w1, ffn_w2, layer, bm=2048, bn=1024)
        g_next = mix_pre_g[layer + 1] if layer + 1 < depth and (layer + 1) % 2 == 1 else None
        xs, h = _ffn_down(a, w2_bf, xs, ffn_post_g[layer], g_next, bm=256)
    return xs.reshape(batch, seq, d)
```

```python
import functools

import jax
import jax.numpy as jnp
from jax import lax
from jax.experimental import pallas as pl
from jax.experimental.pallas import tpu as pltpu

EPS = 1e-6
POOL_WINDOWS = (2, 4, 8, 16)

V7X_SUBLANES = 8
V7X_LANES = 128
V7X_VMEM_LIMIT_CAP = 60000 * 1024

F32 = jnp.float32
BF16 = jnp.bfloat16


def _halo_rows(n):
    return -(-n // V7X_SUBLANES) * V7X_SUBLANES


def _params(semantics, vmem_bytes):
    return pltpu.CompilerParams(
        dimension_semantics=semantics,
        vmem_limit_bytes=min(int(vmem_bytes), V7X_VMEM_LIMIT_CAP))


def _const_spec(shape, index_map):
    return pl.BlockSpec(shape, index_map, pipeline_mode=pl.Buffered(1))


def _cast_weight(dst, src, *, rows=256):
    n = src.shape[0] // rows

    def body(c, carry):
        r = pl.multiple_of(c * rows, rows)
        dst[pl.ds(r, rows), :] = src[pl.ds(r, rows), :].astype(BF16)
        return carry

    lax.fori_loop(0, n, body, 0)


def _rms(x, g):
    ms = jnp.mean(x * x, axis=-1, keepdims=True)
    return x * lax.rsqrt(ms + EPS) * g


def _load_history(buf, i, *, bm, seq, halo):
    first = lax.rem(i * bm, seq) == 0

    @pl.when(first)
    def _():
        buf[0:halo, :] = jnp.zeros((halo, buf.shape[1]), buf.dtype)

    @pl.when(jnp.logical_not(first))
    def _():
        buf[0:halo, :] = buf[bm:bm + halo, :]


def _pool_kernel(x_ref, g_ref, w_ref, pw_ref, ps_ref, o_ref, h_ref, wbf, pwbf, ubuf,
                 *, bm, seq, halo, group, sub):
    i = pl.program_id(0)

    @pl.when(i == 0)
    def _():
        _cast_weight(wbf, w_ref)
        pwbf[...] = pw_ref[...].astype(BF16)

    _load_history(ubuf, i, bm=bm, seq=seq, halo=halo)
    for s0 in range(0, bm, sub):
        rows = slice(s0, s0 + sub)
        h = _rms(x_ref[rows, :], g_ref[...]).astype(BF16)
        h_ref[rows, :] = h
        ubuf[halo + s0:halo + s0 + sub, :] = jnp.dot(h, wbf[...], preferred_element_type=F32)

        pos = lax.rem(i * bm, seq) + s0 + lax.broadcasted_iota(jnp.int32, (sub, 1), 0)
        for g, w in enumerate(POOL_WINDOWS):
            cols = slice(g * group, (g + 1) * group)
            s = ubuf[s0:s0 + halo + sub, cols]
            span = 1
            while span < w:
                s = s + pltpu.roll(s, span, 0)
                span *= 2
            u = ubuf[halo + s0:halo + s0 + sub, cols]
            cnt = jnp.minimum(pos + 1, w).astype(F32)
            pooled = s[halo:, :] / cnt - u
            mixed = jnp.dot(pooled.astype(BF16), pwbf[g], preferred_element_type=F32)
            o_ref[rows, cols] = (mixed * ps_ref[:, cols]).astype(o_ref.dtype)


def _pool_branch(x, g_pre, w_in, pool_w, pool_scale, *, bm, seq):
    m, d = x.shape
    n_groups, group, _ = pool_w.shape
    d_pool = n_groups * group
    halo = _halo_rows(max(POOL_WINDOWS) - 1)
    assert all(w & (w - 1) == 0 for w in POOL_WINDOWS)
    kern = functools.partial(_pool_kernel, bm=bm, seq=seq, halo=halo, group=group,
                             sub=min(bm, 256))
    vmem = (d * d_pool * (4 + 2) + 2 * bm * d * (4 + 2) + 2 * bm * d_pool * 2
            + (halo + bm) * d_pool * 4 + 3 * bm * d_pool * 4 + pool_w.size * 6 + 3 * bm * d * 4)
    return pl.pallas_call(
        kern,
        grid=(m // bm,),
        in_specs=[pl.BlockSpec((bm, d), lambda i: (i, 0)),
                  _const_spec((1, d), lambda i: (0, 0)),
                  _const_spec((d, d_pool), lambda i: (0, 0)),
                  _const_spec(pool_w.shape, lambda i: (0, 0, 0)),
                  _const_spec((1, d_pool), lambda i: (0, 0))],
        out_specs=[pl.BlockSpec((bm, d_pool), lambda i: (i, 0)),
                   pl.BlockSpec((bm, d), lambda i: (i, 0))],
        out_shape=[jax.ShapeDtypeStruct((m, d_pool), BF16),
                   jax.ShapeDtypeStruct((m, d), BF16)],
        scratch_shapes=[pltpu.VMEM((d, d_pool), BF16),
                        pltpu.VMEM(pool_w.shape, BF16),
                        pltpu.VMEM((halo + bm, d_pool), F32)],
        compiler_params=_params(("arbitrary",), vmem),
        name="pool_branch",
    )(x, g_pre.reshape(1, d), w_in, pool_w, pool_scale.reshape(1, d_pool))


def _cast_slab_specs(rows, d, ni):
    index = lambda j, i: (j * ni + i, 0)
    return pl.BlockSpec((rows, d), index), pl.BlockSpec((rows, d), index)


def _glu_conv_kernel(h_ref, wv_ref, wg_ref, cw_ref, cb_ref, wo_ref, o_ref, wobf_ref,
                     wvbf, wgbf, gbuf, phase, *, bm, bn, seq, halo, taps, sub, rows, lanes):
    i = pl.program_id(1)

    @pl.when(i == 0)
    def _():
        _cast_weight(wvbf, wv_ref)
        _cast_weight(wgbf, wg_ref)

    wobf_ref[...] = wo_ref[...].astype(BF16)
    _load_history(gbuf, i, bm=bm, seq=seq, halo=halo)

    max_a = (taps - 1) // V7X_SUBLANES
    lead = (max_a + 1) * V7X_SUBLANES
    for s0 in range(0, bm, sub):
        h = h_ref[s0:s0 + sub, :]
        v = jnp.dot(h, wvbf[...], preferred_element_type=F32)
        gate = jnp.dot(h, wgbf[...], preferred_element_type=F32)
        gbuf[halo + s0:halo + s0 + sub, :] = v * jax.nn.sigmoid(gate)
        for c0 in range(0, bn, lanes):
            cs = slice(c0, c0 + lanes)
            win = gbuf[halo + s0 - lead:halo + s0 + sub, cs]
            for r in range(1, V7X_SUBLANES):
                phase[r - 1, :, cs] = pltpu.roll(win, r, 0)
        for r0 in range(0, sub, rows):
            for c0 in range(0, bn, lanes):
                cs = slice(c0, c0 + lanes)
                acc = jnp.broadcast_to(cb_ref[:, cs], (rows, lanes))
                for s in range(taps):
                    a, r = divmod(s, V7X_SUBLANES)
                    off = r0 + lead - V7X_SUBLANES * a
                    if r == 0:
                        src = gbuf[halo + s0 - lead + off:halo + s0 - lead + off + rows, cs]
                    else:
                        src = phase[r - 1, off:off + rows, cs]
                    acc = acc + cw_ref[taps - 1 - s:taps - s, cs] * src
                o_ref[s0 + r0:s0 + r0 + rows, cs] = acc


def _glu_conv_branch(h, w_in, conv_w, conv_b, w_out, *, d_pool, bm, bn, seq):
    m, d = h.shape
    taps, d_conv = conv_w.shape
    halo = _halo_rows(taps - 1)
    nj, ni = d_conv // bn, m // bm
    v_blk0 = d_pool // bn
    g_blk0 = (d_pool + d_conv) // bn
    ko, do = w_out.shape
    slab = ko // (nj * ni)
    wo_in, wo_out = _cast_slab_specs(slab, do, ni)
    sub = 256
    lead = _halo_rows(taps)
    assert halo >= lead and bm % sub == 0
    kern = functools.partial(_glu_conv_kernel, bm=bm, bn=bn, seq=seq, halo=halo, taps=taps,
                             sub=sub, rows=128, lanes=V7X_LANES)
    phase_shape = (V7X_SUBLANES - 1, lead + sub, bn)
    vmem = (2 * 2 * d * bn * 4 + 2 * d * bn * 2 + 2 * bm * d * 2 + 2 * bm * bn * 4
            + (halo + bm) * bn * 4 + 4 * sub * bn * 4 + 2 * slab * do * 6
            + 4 * phase_shape[0] * phase_shape[1] * bn)
    return pl.pallas_call(
        kern,
        grid=(nj, ni),
        in_specs=[pl.BlockSpec((bm, d), lambda j, i: (i, 0)),
                  pl.BlockSpec((d, bn), lambda j, i: (0, v_blk0 + j)),
                  pl.BlockSpec((d, bn), lambda j, i: (0, g_blk0 + j)),
                  pl.BlockSpec((taps, bn), lambda j, i: (0, j)),
                  pl.BlockSpec((1, bn), lambda j, i: (0, j)),
                  wo_in],
        out_specs=[pl.BlockSpec((bm, bn), lambda j, i: (i, j)), wo_out],
        out_shape=[jax.ShapeDtypeStruct((m, d_conv), F32),
                   jax.ShapeDtypeStruct((ko, do), BF16)],
        scratch_shapes=[pltpu.VMEM((d, bn), BF16),
                        pltpu.VMEM((d, bn), BF16),
                        pltpu.VMEM((halo + bm, bn), F32),
                        pltpu.VMEM(phase_shape, F32)],
        compiler_params=_params(("arbitrary", "arbitrary"), vmem),
        name="glu_conv_branch",
    )(h, w_in, w_in, conv_w, conv_b.reshape(1, d_conv), w_out)


def _residual_norms(m, x_ref, gpost_ref, gnext_ref, xo_ref, ho_ref, rows=slice(None)):
    x_new = x_ref[rows, :] + _rms(m, gpost_ref[...])
    xo_ref[rows, :] = x_new
    if ho_ref is not None:
        ho_ref[rows, :] = _rms(x_new, gnext_ref[...]).astype(ho_ref.dtype)


def _lagged_steps(i, n_tiles, produce, consume):
    @pl.when(i == 0)
    def _():
        produce()

    @pl.when(jnp.logical_and(i > 0, i < n_tiles))
    def _():
        consume()
        produce()

    @pl.when(i == n_tiles)
    def _():
        consume()


def _out_proj_l0_kernel(yp_ref, c_ref, x_ref, w_ref, lng_ref, lnb_ref, gpost_ref, gnext_ref,
                        xo_ref, ho_ref, *, d_pool, bm, sub):
    for s0 in range(0, bm, sub):
        rows = slice(s0, s0 + sub)
        c = c_ref[rows, :]
        mu = jnp.mean(c, axis=-1, keepdims=True)
        cc = c - mu
        var = jnp.mean(cc * cc, axis=-1, keepdims=True)
        ln = cc * lax.rsqrt(var + EPS) * lng_ref[...] + lnb_ref[...]
        y_conv = (ln * jax.nn.sigmoid(ln)).astype(BF16)
        m = jnp.dot(yp_ref[rows, :], w_ref[0:d_pool, :], preferred_element_type=F32)
        m = m + jnp.dot(y_conv, w_ref[d_pool:, :], preferred_element_type=F32)
        _residual_norms(m, x_ref, gpost_ref, gnext_ref, xo_ref, ho_ref, rows)


def _out_proj_l1_kernel(y_ref, x_ref, w_ref, gpost_ref, gnext_ref, xo_ref, ho_ref, *, bm, sub):
    for s0 in range(0, bm, sub):
        rows = slice(s0, s0 + sub)
        m = jnp.dot(y_ref[rows, :], w_ref[...], preferred_element_type=F32)
        _residual_norms(m, x_ref, gpost_ref, gnext_ref, xo_ref, ho_ref, rows)


def _out_proj(ys, x, w_bf, g_post, g_next, ln=None, *, bm, sub=256):
    assert bm % sub == 0
    m, d = x.shape
    k = w_bf.shape[0]
    row = lambda i: (i, 0)
    fixed = lambda i: (0, 0)
    vec = lambda a: a.reshape(1, -1)
    if ln is None:
        (y,) = ys
        kern = functools.partial(_out_proj_l1_kernel, bm=bm, sub=sub)
        ins = [y, x, w_bf, vec(g_post), vec(g_next)]
        in_specs = [pl.BlockSpec((bm, k), row), pl.BlockSpec((bm, d), row),
                    _const_spec((k, d), fixed), _const_spec((1, d), fixed), _const_spec((1, d), fixed)]
        tile_bytes = bm * k * 2
    else:
        y_pool, c = ys
        d_pool, d_conv = y_pool.shape[1], c.shape[1]
        kern = functools.partial(_out_proj_l0_kernel, d_pool=d_pool, bm=bm, sub=sub)
        ins = [y_pool, c, x, w_bf, vec(ln[0]), vec(ln[1]), vec(g_post), vec(g_next)]
        in_specs = [pl.BlockSpec((bm, d_pool), row), pl.BlockSpec((bm, d_conv), row),
                    pl.BlockSpec((bm, d), row), _const_spec((k, d), fixed),
                    _const_spec((1, d_conv), fixed), _const_spec((1, d_conv), fixed),
                    _const_spec((1, d), fixed), _const_spec((1, d), fixed)]
        tile_bytes = bm * d_pool * 2 + bm * d_conv * 4
    vmem = k * d * 2 + 2 * tile_bytes + 2 * bm * d * (4 + 4 + 2) + 5 * bm * d * 4
    return pl.pallas_call(
        kern,
        grid=(m // bm,),
        in_specs=in_specs,
        out_specs=[pl.BlockSpec((bm, d), row), pl.BlockSpec((bm, d), row)],
        out_shape=[jax.ShapeDtypeStruct((m, d), F32), jax.ShapeDtypeStruct((m, d), BF16)],
        compiler_params=_params(("arbitrary",), vmem),
        name="out_proj_l0" if ln is not None else "out_proj_l1",
    )(*ins)


def _ffn_up_kernel(h_ref, w_ref, w2_ref, o_ref, w2bf_ref, wbf, *, bm, sub):
    @pl.when(pl.program_id(1) == 0)
    def _():
        _cast_weight(wbf, w_ref)

    w2bf_ref[...] = w2_ref[...].astype(BF16)
    for s0 in range(0, bm, sub):
        rows = slice(s0, s0 + sub)
        a = jnp.maximum(jnp.dot(h_ref[rows, :], wbf[...], preferred_element_type=F32), 0.0)
        o_ref[rows, :] = (a * a).astype(o_ref.dtype)


def _ffn_up(h, w1, w2, layer, *, bm, bn):
    m, d = h.shape
    f = w1.shape[2]
    nj, ni = f // bn, m // bm
    slab = f // (nj * ni)
    slab_index = lambda j, i: (j * ni + i, 0)
    sub = min(bm, 512)
    vmem = (2 * d * bn * 4 + d * bn * 2 + 2 * bm * d * 2 + 2 * bm * bn * 2 + 3 * sub * bn * 4
            + 2 * slab * d * 6)
    return pl.pallas_call(
        functools.partial(_ffn_up_kernel, bm=bm, sub=sub),
        grid=(nj, ni),
        in_specs=[pl.BlockSpec((bm, d), lambda j, i: (i, 0)),
                  pl.BlockSpec((None, d, bn), lambda j, i: (layer, 0, j)),
                  pl.BlockSpec((None, slab, d), lambda j, i: (layer, j * ni + i, 0))],
        out_specs=[pl.BlockSpec((bm, bn), lambda j, i: (i, j)),
                   pl.BlockSpec((slab, d), slab_index)],
        out_shape=[jax.ShapeDtypeStruct((m, f), BF16),
                   jax.ShapeDtypeStruct((f, d), BF16)],
        scratch_shapes=[pltpu.VMEM((d, bn), BF16)],
        compiler_params=_params(("arbitrary", "arbitrary"), vmem),
        name="ffn_up",
    )(h, w1, w2)


def _ffn_down_kernel(a_ref, w_ref, x_ref, gpost_ref, gnext_ref, xo_ref, *rest, with_next, n_tiles):
    ho_ref = rest[0] if with_next else None
    m_buf = rest[-1]

    def matmul():
        m_buf[...] = jnp.dot(a_ref[...], w_ref[...], preferred_element_type=F32)

    def norms():
        _residual_norms(m_buf[...], x_ref, gpost_ref, gnext_ref, xo_ref, ho_ref)

    _lagged_steps(pl.program_id(0), n_tiles, matmul, norms)


def _ffn_down(a, w2_bf, x, g_post, g_next, *, bm):
    m, f = a.shape
    d = w2_bf.shape[1]
    n_tiles = m // bm
    with_next = g_next is not None
    if not with_next:
        g_next = g_post
    ahead = lambda i: (jnp.minimum(i, n_tiles - 1), 0)
    behind = lambda i: (jnp.maximum(i - 1, 0), 0)
    fixed = lambda i: (0, 0)
    out_specs = [pl.BlockSpec((bm, d), behind)]
    out_shape = [jax.ShapeDtypeStruct((m, d), F32)]
    if with_next:
        out_specs.append(pl.BlockSpec((bm, d), behind))
        out_shape.append(jax.ShapeDtypeStruct((m, d), BF16))
    vmem = f * d * 2 + 2 * bm * f * 2 + 2 * bm * d * (4 + 4 + 2) + 4 * bm * d * 4
    outs = pl.pallas_call(
        functools.partial(_ffn_down_kernel, with_next=with_next, n_tiles=n_tiles),
        grid=(n_tiles + 1,),
        in_specs=[pl.BlockSpec((bm, f), ahead),
                  _const_spec((f, d), fixed),
                  pl.BlockSpec((bm, d), behind),
                  _const_spec((1, d), fixed), _const_spec((1, d), fixed)],
        out_specs=out_specs,
        out_shape=out_shape,
        scratch_shapes=[pltpu.VMEM((bm, d), F32)],
        compiler_params=_params(("arbitrary",), vmem),
        name="ffn_down",
    )(a, w2_bf, x, g_post.reshape(1, d), g_next.reshape(1, d))
    return outs if with_next else (outs[0], None)


def _gated_conv_kernel(h_ref, wb_ref, wc_ref, wu_ref, cw_ref, wo_ref, o_ref, wobf_ref,
                       wbbf, wcbf, wubf, pbuf, *, bm, seq, halo, taps, sub):
    i = pl.program_id(1)

    @pl.when(i == 0)
    def _():
        _cast_weight(wbbf, wb_ref)
        _cast_weight(wcbf, wc_ref)
        _cast_weight(wubf, wu_ref)

    wobf_ref[...] = wo_ref[...].astype(BF16)
    _load_history(pbuf, i, bm=bm, seq=seq, halo=halo)
    cw = cw_ref[...]
    for s0 in range(0, bm, sub):
        h = h_ref[s0:s0 + sub, :]
        zc = jnp.dot(h, wcbf[...], preferred_element_type=F32)
        zu = jnp.dot(h, wubf[...], preferred_element_type=F32)
        pbuf[halo + s0:halo + s0 + sub, :] = zc * zu
        conv = cw[taps - 1:taps, :] * pbuf[halo + s0:halo + s0 + sub, :]
        for k in range(taps - 1):
            start = halo + s0 - (taps - 1) + k
            conv = conv + cw[k:k + 1, :] * pbuf[start:start + sub, :]
        zb = jnp.dot(h, wbbf[...], preferred_element_type=F32)
        o_ref[s0:s0 + sub, :] = (zb * conv).astype(o_ref.dtype)


def _gated_conv_branch(h, w_in, conv_w, w_out, *, bm, bn, seq):
    m, d = h.shape
    taps, d_short = conv_w.shape
    halo = _halo_rows(taps - 1)
    nj, ni = d_short // bn, m // bm
    ko, do = w_out.shape
    slab = ko // (nj * ni)
    wo_in, wo_out = _cast_slab_specs(slab, do, ni)
    kern = functools.partial(_gated_conv_kernel, bm=bm, seq=seq, halo=halo, taps=taps,
                             sub=min(bm, 512))
    vmem = (3 * 2 * d * bn * 4 + 3 * d * bn * 2 + 2 * bm * d * 2 + 2 * bm * bn * 2
            + (halo + bm) * bn * 4 + 5 * bm * bn * 4 + 2 * slab * do * 6)
    return pl.pallas_call(
        kern,
        grid=(nj, ni),
        in_specs=[pl.BlockSpec((bm, d), lambda j, i: (i, 0)),
                  pl.BlockSpec((d, bn), lambda j, i: (0, j)),
                  pl.BlockSpec((d, bn), lambda j, i: (0, nj + j)),
                  pl.BlockSpec((d, bn), lambda j, i: (0, 2 * nj + j)),
                  pl.BlockSpec((taps, bn), lambda j, i: (0, j)),
                  wo_in],
        out_specs=[pl.BlockSpec((bm, bn), lambda j, i: (i, j)), wo_out],
        out_shape=[jax.ShapeDtypeStruct((m, d_short), BF16),
                   jax.ShapeDtypeStruct((ko, do), BF16)],
        scratch_shapes=[pltpu.VMEM((d, bn), BF16), pltpu.VMEM((d, bn), BF16),
                        pltpu.VMEM((d, bn), BF16), pltpu.VMEM((halo + bm, bn), F32)],
        compiler_params=_params(("arbitrary", "arbitrary"), vmem),
        name="gated_conv_branch",
    )(h, w_in, w_in, w_in, conv_w, w_out)


def kernel(x, mix_pre_g, mix_post_g, ffn_pre_g, ffn_post_g, ab_w_in, pool_w, pool_scale,
           conv_w, conv_b, conv_ln_g, conv_ln_b, ab_w_out, sc_w_in, sc_conv_w, sc_w_out,
           ffn_w1, ffn_w2):
    batch, seq, d = x.shape
    depth = mix_pre_g.shape[0]
    d_pool = pool_scale.shape[1]
    xs = x.reshape(batch * seq, d)
    h = None
    for layer in range(depth):
        i = layer // 2
        if layer % 2 == 0:
            y_pool, h = _pool_branch(xs, mix_pre_g[layer], ab_w_in[i], pool_w[i], pool_scale[i],
                                     bm=512, seq=seq)
            c, w_out_bf = _glu_conv_branch(h, ab_w_in[i], conv_w[i], conv_b[i], ab_w_out[i],
                                           d_pool=d_pool, bm=1024, bn=512, seq=seq)
            xs, h = _out_proj((y_pool, c), xs, w_out_bf, mix_post_g[layer], ffn_pre_g[layer],
                              ln=(conv_ln_g[i], conv_ln_b[i]), bm=512)
        else:
            y, w_out_bf = _gated_conv_branch(h, sc_w_in[i], sc_conv_w[i], sc_w_out[i],
                                             bm=1024, bn=512, seq=seq)
            xs, h = _out_proj((y,), xs, w_out_bf, mix_post_g[layer], ffn_pre_g[layer], bm=512)
        a, w2_bf = _ffn_up(h, ffn_w1, ffn_w2, layer, bm=2048, bn=1024)
        g_next = mix_pre_g[layer + 1] if layer + 1 < depth and (layer + 1) % 2 == 1 else None
        xs, h = _ffn_down(a, w2_bf, xs, ffn_post_g[layer], g_next, bm=256)
    return xs.reshape(batch, seq, d)
```

```python
import functools

import jax
import jax.numpy as jnp
from jax import lax
from jax.experimental import pallas as pl
from jax.experimental.pallas import tpu as pltpu

EPS = 1e-6
POOL_WINDOWS = (2, 4, 8, 16)

V7X_SUBLANES = 8
V7X_LANES = 128
V7X_VMEM_LIMIT_CAP = 60000 * 1024

F32 = jnp.float32
BF16 = jnp.bfloat16


def _halo_rows(n):
    return -(-n // V7X_SUBLANES) * V7X_SUBLANES


def _params(semantics, vmem_bytes):
    return pltpu.CompilerParams(
        dimension_semantics=semantics,
        vmem_limit_bytes=min(int(vmem_bytes), V7X_VMEM_LIMIT_CAP))


def _const_spec(shape, index_map):
    return pl.BlockSpec(shape, index_map, pipeline_mode=pl.Buffered(1))


def _cast_weight(dst, src, *, rows=256):
    n = src.shape[0] // rows

    def body(c, carry):
        r = pl.multiple_of(c * rows, rows)
        dst[pl.ds(r, rows), :] = src[pl.ds(r, rows), :].astype(BF16)
        return carry

    lax.fori_loop(0, n, body, 0)


def _rms(x, g):
    ms = jnp.mean(x * x, axis=-1, keepdims=True)
    return x * lax.rsqrt(ms + EPS) * g


def _load_history(buf, i, *, bm, seq, halo):
    first = lax.rem(i * bm, seq) == 0

    @pl.when(first)
    def _():
        buf[..., 0:halo, :] = jnp.zeros(buf.shape[:-2] + (halo, buf.shape[-1]), buf.dtype)

    @pl.when(jnp.logical_not(first))
    def _():
        buf[..., 0:halo, :] = buf[..., bm:bm + halo, :]


def _pool_kernel(x_ref, g_ref, w_ref, pw_ref, ps_ref, o_ref, h_ref, wbf, pwbf, ubuf,
                 *, bm, seq, halo, group, sub):
    i = pl.program_id(0)

    @pl.when(i == 0)
    def _():
        _cast_weight(wbf, w_ref)
        pwbf[...] = pw_ref[...].astype(BF16)

    _load_history(ubuf, i, bm=bm, seq=seq, halo=halo)
    for s0 in range(0, bm, sub):
        rows = slice(s0, s0 + sub)
        h = _rms(x_ref[rows, :], g_ref[...]).astype(BF16)
        h_ref[rows, :] = h
        ubuf[halo + s0:halo + s0 + sub, :] = jnp.dot(h, wbf[...], preferred_element_type=F32)

        pos = lax.rem(i * bm, seq) + s0 + lax.broadcasted_iota(jnp.int32, (sub, 1), 0)
        for g, w in enumerate(POOL_WINDOWS):
            cols = slice(g * group, (g + 1) * group)
            s = ubuf[s0:s0 + halo + sub, cols]
            span = 1
            while span < w:
                s = s + pltpu.roll(s, span, 0)
                span *= 2
            u = ubuf[halo + s0:halo + s0 + sub, cols]
            cnt = jnp.minimum(pos + 1, w).astype(F32)
            pooled = s[halo:, :] / cnt - u
            mixed = jnp.dot(pooled.astype(BF16), pwbf[g], preferred_element_type=F32)
            o_ref[rows, cols] = (mixed * ps_ref[:, cols]).astype(o_ref.dtype)


def _pool_branch(x, g_pre, w_in, pool_w, pool_scale, *, bm, seq):
    m, d = x.shape
    n_groups, group, _ = pool_w.shape
    d_pool = n_groups * group
    halo = _halo_rows(max(POOL_WINDOWS) - 1)
    assert all(w & (w - 1) == 0 for w in POOL_WINDOWS)
    kern = functools.partial(_pool_kernel, bm=bm, seq=seq, halo=halo, group=group,
                             sub=min(bm, 256))
    vmem = (d * d_pool * (4 + 2) + 2 * bm * d * (4 + 2) + 2 * bm * d_pool * 2
            + (halo + bm) * d_pool * 4 + 3 * bm * d_pool * 4 + pool_w.size * 6 + 3 * bm * d * 4)
    return pl.pallas_call(
        kern,
        grid=(m // bm,),
        in_specs=[pl.BlockSpec((bm, d), lambda i: (i, 0)),
                  _const_spec((1, d), lambda i: (0, 0)),
                  _const_spec((d, d_pool), lambda i: (0, 0)),
                  _const_spec(pool_w.shape, lambda i: (0, 0, 0)),
                  _const_spec((1, d_pool), lambda i: (0, 0))],
        out_specs=[pl.BlockSpec((bm, d_pool), lambda i: (i, 0)),
                   pl.BlockSpec((bm, d), lambda i: (i, 0))],
        out_shape=[jax.ShapeDtypeStruct((m, d_pool), BF16),
                   jax.ShapeDtypeStruct((m, d), BF16)],
        scratch_shapes=[pltpu.VMEM((d, d_pool), BF16),
                        pltpu.VMEM(pool_w.shape, BF16),
                        pltpu.VMEM((halo + bm, d_pool), F32)],
        compiler_params=_params(("arbitrary",), vmem),
        name="pool_branch",
    )(x, g_pre.reshape(1, d), w_in, pool_w, pool_scale.reshape(1, d_pool))


def _cast_slab_specs(rows, d, ni):
    index = lambda j, i: (j * ni + i, 0)
    return pl.BlockSpec((rows, d), index), pl.BlockSpec((rows, d), index)


def _glu_conv_kernel(h_ref, wv_ref, wg_ref, cw_ref, cb_ref, wo_ref, o_ref, wobf_ref,
                     wvbf, wgbf, gbuf, wrap, obuf, *, bm, bn, seq, halo, taps, sub):
    i = pl.program_id(1)

    @pl.when(i == 0)
    def _():
        _cast_weight(wvbf, wv_ref)
        _cast_weight(wgbf, wg_ref)

    wobf_ref[...] = wo_ref[...].astype(BF16)
    _load_history(gbuf, i, bm=bm, seq=seq, halo=halo)

    n_lane_tiles = bn // V7X_LANES
    for s0 in range(0, bm, sub):
        h = h_ref[s0:s0 + sub, :]
        v = jnp.dot(h, wvbf[...], preferred_element_type=F32)
        gate = jnp.dot(h, wgbf[...], preferred_element_type=F32)
        g = v * jax.nn.sigmoid(gate)
        for c in range(n_lane_tiles):
            gbuf[c, halo + s0:halo + s0 + sub, :] = g[:, c * V7X_LANES:(c + 1) * V7X_LANES]

    span = (halo + bm) // V7X_SUBLANES
    first_wrapped = span - (taps - 1)
    for c in range(n_lane_tiles):
        cs = slice(c * V7X_LANES, (c + 1) * V7X_LANES)
        for u in range(first_wrapped, span):
            k = u - first_wrapped
            wrap[c, k * V7X_SUBLANES:(k + 1) * V7X_SUBLANES, :] = pltpu.roll(
                gbuf[c, pl.ds(u, V7X_SUBLANES, stride=span), :], 1, 0)
        for v0 in range(span):
            acc = jnp.broadcast_to(cb_ref[:, cs], (V7X_SUBLANES, V7X_LANES))
            for s in range(taps):
                if v0 >= s:
                    src = gbuf[c, pl.ds(v0 - s, V7X_SUBLANES, stride=span), :]
                else:
                    k = v0 - s + taps - 1
                    src = wrap[c, k * V7X_SUBLANES:(k + 1) * V7X_SUBLANES, :]
                acc = acc + cw_ref[taps - 1 - s:taps - s, cs] * src
            obuf[c, pl.ds(v0, V7X_SUBLANES, stride=span), :] = acc
        o_ref[:, cs] = obuf[c, halo:, :]


def _glu_conv_branch(h, w_in, conv_w, conv_b, w_out, *, d_pool, bm, bn, seq):
    m, d = h.shape
    taps, d_conv = conv_w.shape
    halo = _halo_rows(taps - 1)
    nj, ni = d_conv // bn, m // bm
    v_blk0 = d_pool // bn
    g_blk0 = (d_pool + d_conv) // bn
    ko, do = w_out.shape
    slab = ko // (nj * ni)
    wo_in, wo_out = _cast_slab_specs(slab, do, ni)
    sub = 256
    assert bm % sub == 0 and bn % V7X_LANES == 0 and halo >= taps - 1
    kern = functools.partial(_glu_conv_kernel, bm=bm, bn=bn, seq=seq, halo=halo, taps=taps, sub=sub)
    n_lane_tiles = bn // V7X_LANES
    win_shape = (n_lane_tiles, halo + bm, V7X_LANES)
    wrap_shape = (n_lane_tiles, (taps - 1) * V7X_SUBLANES, V7X_LANES)
    vmem = (2 * 2 * d * bn * 4 + 2 * d * bn * 2 + 2 * bm * d * 2 + 2 * bm * bn * 4
            + 2 * (halo + bm) * bn * 4 + 4 * sub * bn * 4 + 2 * slab * do * 6
            + 4 * wrap_shape[0] * wrap_shape[1] * wrap_shape[2])
    return pl.pallas_call(
        kern,
        grid=(nj, ni),
        in_specs=[pl.BlockSpec((bm, d), lambda j, i: (i, 0)),
                  pl.BlockSpec((d, bn), lambda j, i: (0, v_blk0 + j)),
                  pl.BlockSpec((d, bn), lambda j, i: (0, g_blk0 + j)),
                  pl.BlockSpec((taps, bn), lambda j, i: (0, j)),
                  pl.BlockSpec((1, bn), lambda j, i: (0, j)),
                  wo_in],
        out_specs=[pl.BlockSpec((bm, bn), lambda j, i: (i, j)), wo_out],
        out_shape=[jax.ShapeDtypeStruct((m, d_conv), F32),
                   jax.ShapeDtypeStruct((ko, do), BF16)],
        scratch_shapes=[pltpu.VMEM((d, bn), BF16),
                        pltpu.VMEM((d, bn), BF16),
                        pltpu.VMEM(win_shape, F32),
                        pltpu.VMEM(wrap_shape, F32),
                        pltpu.VMEM(win_shape, F32)],
        compiler_params=_params(("arbitrary", "arbitrary"), vmem),
        name="glu_conv_branch",
    )(h, w_in, w_in, conv_w, conv_b.reshape(1, d_conv), w_out)


def _residual_norms(m, x_ref, gpost_ref, gnext_ref, xo_ref, ho_ref, rows=slice(None)):
    x_new = x_ref[rows, :] + _rms(m, gpost_ref[...])
    xo_ref[rows, :] = x_new
    if ho_ref is not None:
        ho_ref[rows, :] = _rms(x_new, gnext_ref[...]).astype(ho_ref.dtype)


def _lagged_steps(i, n_tiles, produce, consume, produce_first=None):
    @pl.when(i == 0)
    def _():
        (produce_first or produce)()

    @pl.when(jnp.logical_and(i > 0, i < n_tiles))
    def _():
        consume()
        produce()

    @pl.when(i == n_tiles)
    def _():
        consume()


def _out_proj_l0_kernel(yp_ref, c_ref, x_ref, w_ref, lng_ref, lnb_ref, gpost_ref, gnext_ref,
                        xo_ref, ho_ref, *, d_pool, bm, sub):
    for s0 in range(0, bm, sub):
        rows = slice(s0, s0 + sub)
        c = c_ref[rows, :]
        mu = jnp.mean(c, axis=-1, keepdims=True)
        cc = c - mu
        var = jnp.mean(cc * cc, axis=-1, keepdims=True)
        ln = cc * lax.rsqrt(var + EPS) * lng_ref[...] + lnb_ref[...]
        y_conv = (ln * jax.nn.sigmoid(ln)).astype(BF16)
        m = jnp.dot(yp_ref[rows, :], w_ref[0:d_pool, :], preferred_element_type=F32)
        m = m + jnp.dot(y_conv, w_ref[d_pool:, :], preferred_element_type=F32)
        _residual_norms(m, x_ref, gpost_ref, gnext_ref, xo_ref, ho_ref, rows)


def _out_proj_l1_kernel(y_ref, x_ref, w_ref, gpost_ref, gnext_ref, xo_ref, ho_ref, *, bm, sub):
    for s0 in range(0, bm, sub):
        rows = slice(s0, s0 + sub)
        m = jnp.dot(y_ref[rows, :], w_ref[...], preferred_element_type=F32)
        _residual_norms(m, x_ref, gpost_ref, gnext_ref, xo_ref, ho_ref, rows)


def _out_proj(ys, x, w_bf, g_post, g_next, ln=None, *, bm, sub=256):
    assert bm % sub == 0
    m, d = x.shape
    k = w_bf.shape[0]
    row = lambda i: (i, 0)
    fixed = lambda i: (0, 0)
    vec = lambda a: a.reshape(1, -1)
    if ln is None:
        (y,) = ys
        kern = functools.partial(_out_proj_l1_kernel, bm=bm, sub=sub)
        ins = [y, x, w_bf, vec(g_post), vec(g_next)]
        in_specs = [pl.BlockSpec((bm, k), row), pl.BlockSpec((bm, d), row),
                    _const_spec((k, d), fixed), _const_spec((1, d), fixed), _const_spec((1, d), fixed)]
        tile_bytes = bm * k * 2
    else:
        y_pool, c = ys
        d_pool, d_conv = y_pool.shape[1], c.shape[1]
        kern = functools.partial(_out_proj_l0_kernel, d_pool=d_pool, bm=bm, sub=sub)
        ins = [y_pool, c, x, w_bf, vec(ln[0]), vec(ln[1]), vec(g_post), vec(g_next)]
        in_specs = [pl.BlockSpec((bm, d_pool), row), pl.BlockSpec((bm, d_conv), row),
                    pl.BlockSpec((bm, d), row), _const_spec((k, d), fixed),
                    _const_spec((1, d_conv), fixed), _const_spec((1, d_conv), fixed),
                    _const_spec((1, d), fixed), _const_spec((1, d), fixed)]
        tile_bytes = bm * d_pool * 2 + bm * d_conv * 4
    vmem = k * d * 2 + 2 * tile_bytes + 2 * bm * d * (4 + 4 + 2) + 5 * bm * d * 4
    return pl.pallas_call(
        kern,
        grid=(m // bm,),
        in_specs=in_specs,
        out_specs=[pl.BlockSpec((bm, d), row), pl.BlockSpec((bm, d), row)],
        out_shape=[jax.ShapeDtypeStruct((m, d), F32), jax.ShapeDtypeStruct((m, d), BF16)],
        compiler_params=_params(("arbitrary",), vmem),
        name="out_proj_l0" if ln is not None else "out_proj_l1",
    )(*ins)


def _ffn_up_kernel(h_ref, w_ref, w2_ref, o_ref, w2bf_ref, wbf, *, bm, sub):
    @pl.when(pl.program_id(1) == 0)
    def _():
        _cast_weight(wbf, w_ref)

    w2bf_ref[...] = w2_ref[...].astype(BF16)
    for s0 in range(0, bm, sub):
        rows = slice(s0, s0 + sub)
        a = jnp.maximum(jnp.dot(h_ref[rows, :], wbf[...], preferred_element_type=F32), 0.0)
        o_ref[rows, :] = (a * a).astype(o_ref.dtype)


def _ffn_up(h, w1, w2, layer, *, bm, bn):
    m, d = h.shape
    f = w1.shape[2]
    nj, ni = f // bn, m // bm
    slab = f // (nj * ni)
    slab_index = lambda j, i: (j * ni + i, 0)
    sub = min(bm, 512)
    vmem = (2 * d * bn * 4 + d * bn * 2 + 2 * bm * d * 2 + 2 * bm * bn * 2 + 3 * sub * bn * 4
            + 2 * slab * d * 6)
    return pl.pallas_call(
        functools.partial(_ffn_up_kernel, bm=bm, sub=sub),
        grid=(nj, ni),
        in_specs=[pl.BlockSpec((bm, d), lambda j, i: (i, 0)),
                  pl.BlockSpec((None, d, bn), lambda j, i: (layer, 0, j)),
                  pl.BlockSpec((None, slab, d), lambda j, i: (layer, j * ni + i, 0))],
        out_specs=[pl.BlockSpec((bm, bn), lambda j, i: (i, j)),
                   pl.BlockSpec((slab, d), slab_index)],
        out_shape=[jax.ShapeDtypeStruct((m, f), BF16),
                   jax.ShapeDtypeStruct((f, d), BF16)],
        scratch_shapes=[pltpu.VMEM((d, bn), BF16)],
        compiler_params=_params(("arbitrary", "arbitrary"), vmem),
        name="ffn_up",
    )(h, w1, w2)


def _ffn_down_kernel(a_ref, w_hbm, x_ref, gpost_ref, gnext_ref, xo_ref, *rest, with_next, n_tiles,
                     n_chunks):
    ho_ref = rest[0] if with_next else None
    m_buf, w_ref, sems = rest[-3:]
    kc = w_ref.shape[0] // n_chunks

    def chunk_copy(c):
        rows = pl.ds(c * kc, kc)
        return pltpu.make_async_copy(w_hbm.at[rows, :], w_ref.at[rows, :], sems.at[c])

    def first_matmul():
        for c in range(n_chunks):
            chunk_copy(c).start()
        for c in range(n_chunks):
            chunk_copy(c).wait()
            part = jnp.dot(a_ref[:, c * kc:(c + 1) * kc], w_ref[c * kc:(c + 1) * kc, :],
                           preferred_element_type=F32)
            if c == 0:
                m_buf[...] = part
            else:
                m_buf[...] += part

    def matmul():
        m_buf[...] = jnp.dot(a_ref[...], w_ref[...], preferred_element_type=F32)

    def norms():
        _residual_norms(m_buf[...], x_ref, gpost_ref, gnext_ref, xo_ref, ho_ref)

    _lagged_steps(pl.program_id(0), n_tiles, matmul, norms, produce_first=first_matmul)


def _ffn_down(a, w2_bf, x, g_post, g_next, *, bm):
    m, f = a.shape
    d = w2_bf.shape[1]
    n_tiles = m // bm
    with_next = g_next is not None
    if not with_next:
        g_next = g_post
    ahead = lambda i: (jnp.minimum(i, n_tiles - 1), 0)
    behind = lambda i: (jnp.maximum(i - 1, 0), 0)
    fixed = lambda i: (0, 0)
    out_specs = [pl.BlockSpec((bm, d), behind)]
    out_shape = [jax.ShapeDtypeStruct((m, d), F32)]
    if with_next:
        out_specs.append(pl.BlockSpec((bm, d), behind))
        out_shape.append(jax.ShapeDtypeStruct((m, d), BF16))
    vmem = f * d * 2 + 2 * bm * f * 2 + 2 * bm * d * (4 + 4 + 2) + 4 * bm * d * 4
    n_chunks = 8
    assert f % n_chunks == 0
    outs = pl.pallas_call(
        functools.partial(_ffn_down_kernel, with_next=with_next, n_tiles=n_tiles,
                          n_chunks=n_chunks),
        grid=(n_tiles + 1,),
        in_specs=[pl.BlockSpec((bm, f), ahead),
                  pl.BlockSpec(memory_space=pl.ANY),
                  pl.BlockSpec((bm, d), behind),
                  _const_spec((1, d), fixed), _const_spec((1, d), fixed)],
        out_specs=out_specs,
        out_shape=out_shape,
        scratch_shapes=[pltpu.VMEM((bm, d), F32),
                        pltpu.VMEM((f, d), BF16),
                        pltpu.SemaphoreType.DMA((n_chunks,))],
        compiler_params=_params(("arbitrary",), vmem),
        name="ffn_down",
    )(a, w2_bf, x, g_post.reshape(1, d), g_next.reshape(1, d))
    return outs if with_next else (outs[0], None)


def _gated_conv_kernel(h_ref, wb_ref, wc_ref, wu_ref, cw_ref, wo_ref, o_ref, wobf_ref,
                       wbbf, wcbf, wubf, pbuf, *, bm, seq, halo, taps, sub):
    i = pl.program_id(1)

    @pl.when(i == 0)
    def _():
        _cast_weight(wbbf, wb_ref)
        _cast_weight(wcbf, wc_ref)
        _cast_weight(wubf, wu_ref)

    wobf_ref[...] = wo_ref[...].astype(BF16)
    _load_history(pbuf, i, bm=bm, seq=seq, halo=halo)
    cw = cw_ref[...]
    for s0 in range(0, bm, sub):
        h = h_ref[s0:s0 + sub, :]
        zc = jnp.dot(h, wcbf[...], preferred_element_type=F32)
        zu = jnp.dot(h, wubf[...], preferred_element_type=F32)
        pbuf[halo + s0:halo + s0 + sub, :] = zc * zu
        conv = cw[taps - 1:taps, :] * pbuf[halo + s0:halo + s0 + sub, :]
        for k in range(taps - 1):
            start = halo + s0 - (taps - 1) + k
            conv = conv + cw[k:k + 1, :] * pbuf[start:start + sub, :]
        zb = jnp.dot(h, wbbf[...], preferred_element_type=F32)
        o_ref[s0:s0 + sub, :] = (zb * conv).astype(o_ref.dtype)


def _gated_conv_branch(h, w_in, conv_w, w_out, *, bm, bn, seq):
    m, d = h.shape
    taps, d_short = conv_w.shape
    halo = _halo_rows(taps - 1)
    nj, ni = d_short // bn, m // bm
    ko, do = w_out.shape
    slab = ko // (nj * ni)
    wo_in, wo_out = _cast_slab_specs(slab, do, ni)
    kern = functools.partial(_gated_conv_kernel, bm=bm, seq=seq, halo=halo, taps=taps,
                             sub=min(bm, 512))
    vmem = (3 * 2 * d * bn * 4 + 3 * d * bn * 2 + 2 * bm * d * 2 + 2 * bm * bn * 2
            + (halo + bm) * bn * 4 + 5 * bm * bn * 4 + 2 * slab * do * 6)
    return pl.pallas_call(
        kern,
        grid=(nj, ni),
        in_specs=[pl.BlockSpec((bm, d), lambda j, i: (i, 0)),
                  pl.BlockSpec((d, bn), lambda j, i: (0, j)),
                  pl.BlockSpec((d, bn), lambda j, i: (0, nj + j)),
                  pl.BlockSpec((d, bn), lambda j, i: (0, 2 * nj + j)),
                  pl.BlockSpec((taps, bn), lambda j, i: (0, j)),
                  wo_in],
        out_specs=[pl.BlockSpec((bm, bn), lambda j, i: (i, j)), wo_out],
        out_shape=[jax.ShapeDtypeStruct((m, d_short), BF16),
                   jax.ShapeDtypeStruct((ko, do), BF16)],
        scratch_shapes=[pltpu.VMEM((d, bn), BF16), pltpu.VMEM((d, bn), BF16),
                        pltpu.VMEM((d, bn), BF16), pltpu.VMEM((halo + bm, bn), F32)],
        compiler_params=_params(("arbitrary", "arbitrary"), vmem),
        name="gated_conv_branch",
    )(h, w_in, w_in, w_in, conv_w, w_out)


def kernel(x, mix_pre_g, mix_post_g, ffn_pre_g, ffn_post_g, ab_w_in, pool_w, pool_scale,
           conv_w, conv_b, conv_ln_g, conv_ln_b, ab_w_out, sc_w_in, sc_conv_w, sc_w_out,
           ffn_w1, ffn_w2):
    batch, seq, d = x.shape
    depth = mix_pre_g.shape[0]
    d_pool = pool_scale.shape[1]
    xs = x.reshape(batch * seq, d)
    h = None
    for layer in range(depth):
        i = layer // 2
        if layer % 2 == 0:
            y_pool, h = _pool_branch(xs, mix_pre_g[layer], ab_w_in[i], pool_w[i], pool_scale[i],
                                     bm=512, seq=seq)
            c, w_out_bf = _glu_conv_branch(h, ab_w_in[i], conv_w[i], conv_b[i], ab_w_out[i],
                                           d_pool=d_pool, bm=1024, bn=512, seq=seq)
            xs, h = _out_proj((y_pool, c), xs, w_out_bf, mix_post_g[layer], ffn_pre_g[layer],
                              ln=(conv_ln_g[i], conv_ln_b[i]), bm=512)
        else:
            y, w_out_bf = _gated_conv_branch(h, sc_w_in[i], sc_conv_w[i], sc_w_out[i],
                                             bm=1024, bn=512, seq=seq)
            xs, h = _out_proj((y,), xs, w_out_bf, mix_post_g[layer], ffn_pre_g[layer], bm=512)
        a, w2_bf = _ffn_up(h, ffn_w1, ffn_w2, layer, bm=2048, bn=1024)
        g_next = mix_pre_g[layer + 1] if layer + 1 < depth and (layer + 1) % 2 == 1 else None
        xs, h = _ffn_down(a, w2_bf, xs, ffn_post_g[layer], g_next, bm=256)
    return xs.reshape(batch, seq, d)
```

```python
import functools

import jax
import jax.numpy as jnp
from jax import lax
from jax.experimental import pallas as pl
from jax.experimental.pallas import tpu as pltpu

EPS = 1e-6
POOL_WINDOWS = (2, 4, 8, 16)

V7X_SUBLANES = 8
V7X_LANES = 128
V7X_VMEM_LIMIT_CAP = 60000 * 1024

F32 = jnp.float32
BF16 = jnp.bfloat16


def _halo_rows(n):
    return -(-n // V7X_SUBLANES) * V7X_SUBLANES


def _params(semantics, vmem_bytes):
    return pltpu.CompilerParams(
        dimension_semantics=semantics,
        vmem_limit_bytes=min(int(vmem_bytes), V7X_VMEM_LIMIT_CAP))


def _const_spec(shape, index_map):
    return pl.BlockSpec(shape, index_map, pipeline_mode=pl.Buffered(1))


def _cast_weight(dst, src, *, col0=0, rows=256):
    n = src.shape[0] // rows
    cols = slice(col0, col0 + src.shape[1])

    def body(c, carry):
        r = pl.multiple_of(c * rows, rows)
        dst[pl.ds(r, rows), cols] = src[pl.ds(r, rows), :].astype(BF16)
        return carry

    lax.fori_loop(0, n, body, 0)


def _rms(x, g):
    ms = jnp.mean(x * x, axis=-1, keepdims=True)
    return x * lax.rsqrt(ms + EPS) * g


def _load_history(buf, i, *, bm, seq, halo):
    first = lax.rem(i * bm, seq) == 0

    @pl.when(first)
    def _():
        buf[..., 0:halo, :] = jnp.zeros(buf.shape[:-2] + (halo, buf.shape[-1]), buf.dtype)

    @pl.when(jnp.logical_not(first))
    def _():
        buf[..., 0:halo, :] = buf[..., bm:bm + halo, :]


def _pool_kernel(x_ref, g_ref, w_ref, pw_ref, ps_ref, o_ref, h_ref, wbf, pwbf, ubuf,
                 *, bm, seq, halo, group, sub):
    i = pl.program_id(0)

    @pl.when(i == 0)
    def _():
        _cast_weight(wbf, w_ref)
        pwbf[...] = pw_ref[...].astype(BF16)

    _load_history(ubuf, i, bm=bm, seq=seq, halo=halo)
    for s0 in range(0, bm, sub):
        rows = slice(s0, s0 + sub)
        h = _rms(x_ref[rows, :], g_ref[...]).astype(BF16)
        h_ref[rows, :] = h
        ubuf[halo + s0:halo + s0 + sub, :] = jnp.dot(h, wbf[...], preferred_element_type=F32)

        pos = lax.rem(i * bm, seq) + s0 + lax.broadcasted_iota(jnp.int32, (sub, 1), 0)
        for g, w in enumerate(POOL_WINDOWS):
            cols = slice(g * group, (g + 1) * group)
            s = ubuf[s0:s0 + halo + sub, cols]
            span = 1
            while span < w:
                s = s + pltpu.roll(s, span, 0)
                span *= 2
            u = ubuf[halo + s0:halo + s0 + sub, cols]
            cnt = jnp.minimum(pos + 1, w).astype(F32)
            pooled = s[halo:, :] / cnt - u
            mixed = jnp.dot(pooled.astype(BF16), pwbf[g], preferred_element_type=F32)
            o_ref[rows, cols] = (mixed * ps_ref[:, cols]).astype(o_ref.dtype)


def _pool_branch(x, g_pre, w_in, pool_w, pool_scale, *, bm, seq):
    m, d = x.shape
    n_groups, group, _ = pool_w.shape
    d_pool = n_groups * group
    halo = _halo_rows(max(POOL_WINDOWS) - 1)
    assert all(w & (w - 1) == 0 for w in POOL_WINDOWS)
    assert seq % bm == 0 and m % seq == 0
    kern = functools.partial(_pool_kernel, bm=bm, seq=seq, halo=halo, group=group,
                             sub=min(bm, 256))
    vmem = (d * d_pool * (4 + 2) + 2 * bm * d * (4 + 2) + 2 * bm * d_pool * 2
            + (halo + bm) * d_pool * 4 + 3 * bm * d_pool * 4 + pool_w.size * 6 + 3 * bm * d * 4)
    return pl.pallas_call(
        kern,
        grid=(m // bm,),
        in_specs=[pl.BlockSpec((bm, d), lambda i: (i, 0)),
                  _const_spec((1, d), lambda i: (0, 0)),
                  _const_spec((d, d_pool), lambda i: (0, 0)),
                  _const_spec(pool_w.shape, lambda i: (0, 0, 0)),
                  _const_spec((1, d_pool), lambda i: (0, 0))],
        out_specs=[pl.BlockSpec((bm, d_pool), lambda i: (i, 0)),
                   pl.BlockSpec((bm, d), lambda i: (i, 0))],
        out_shape=[jax.ShapeDtypeStruct((m, d_pool), BF16),
                   jax.ShapeDtypeStruct((m, d), BF16)],
        scratch_shapes=[pltpu.VMEM((d, d_pool), BF16),
                        pltpu.VMEM(pool_w.shape, BF16),
                        pltpu.VMEM((halo + bm, d_pool), F32)],
        compiler_params=_params(("arbitrary",), vmem),
        name="pool_branch",
    )(x, g_pre.reshape(1, d), w_in, pool_w, pool_scale.reshape(1, d_pool))


def _cast_slab_specs(rows, d, ni):
    index = lambda j, i: (j * ni + i, 0)
    return pl.BlockSpec((rows, d), index), pl.BlockSpec((rows, d), index)


def _glu_conv_kernel(h_ref, wv_ref, wg_ref, cw_ref, cb_ref, wo_ref, o_ref, wobf_ref,
                     wall, gbuf, wrap, obuf, *, bm, bn, seq, halo, taps, sub):
    i = pl.program_id(1)

    @pl.when(i == 0)
    def _():
        _cast_weight(wall, wv_ref, col0=0)
        _cast_weight(wall, wg_ref, col0=bn)

    wobf_ref[...] = wo_ref[...].astype(BF16)
    _load_history(gbuf, i, bm=bm, seq=seq, halo=halo)

    n_lane_tiles = bn // V7X_LANES
    for s0 in range(0, bm, sub):
        z = jnp.dot(h_ref[s0:s0 + sub, :], wall[...], preferred_element_type=F32)
        g = z[:, 0:bn] * jax.nn.sigmoid(z[:, bn:])
        for c in range(n_lane_tiles):
            gbuf[c, halo + s0:halo + s0 + sub, :] = g[:, c * V7X_LANES:(c + 1) * V7X_LANES]

    span = (halo + bm) // V7X_SUBLANES
    first_wrapped = span - (taps - 1)
    for c in range(n_lane_tiles):
        cs = slice(c * V7X_LANES, (c + 1) * V7X_LANES)
        for u in range(first_wrapped, span):
            k = u - first_wrapped
            wrap[c, k * V7X_SUBLANES:(k + 1) * V7X_SUBLANES, :] = pltpu.roll(
                gbuf[c, pl.ds(u, V7X_SUBLANES, stride=span), :], 1, 0)
        for v0 in range(span):
            acc = jnp.broadcast_to(cb_ref[:, cs], (V7X_SUBLANES, V7X_LANES))
            for s in range(taps):
                if v0 >= s:
                    src = gbuf[c, pl.ds(v0 - s, V7X_SUBLANES, stride=span), :]
                else:
                    k = v0 - s + taps - 1
                    src = wrap[c, k * V7X_SUBLANES:(k + 1) * V7X_SUBLANES, :]
                acc = acc + cw_ref[taps - 1 - s:taps - s, cs] * src
            obuf[c, pl.ds(v0, V7X_SUBLANES, stride=span), :] = acc
        o_ref[:, cs] = obuf[c, halo:, :]


def _glu_conv_branch(h, w_in, conv_w, conv_b, w_out, *, d_pool, bm, bn, seq):
    m, d = h.shape
    taps, d_conv = conv_w.shape
    halo = _halo_rows(taps - 1)
    nj, ni = d_conv // bn, m // bm
    v_blk0 = d_pool // bn
    g_blk0 = (d_pool + d_conv) // bn
    ko, do = w_out.shape
    slab = ko // (nj * ni)
    wo_in, wo_out = _cast_slab_specs(slab, do, ni)
    sub = 256
    assert bm % sub == 0 and bn % V7X_LANES == 0 and halo >= taps - 1
    assert seq % bm == 0 and m % seq == 0
    assert ko % (nj * ni) == 0 and d_pool % bn == 0 and d_conv % bn == 0
    kern = functools.partial(_glu_conv_kernel, bm=bm, bn=bn, seq=seq, halo=halo, taps=taps, sub=sub)
    n_lane_tiles = bn // V7X_LANES
    win_shape = (n_lane_tiles, halo + bm, V7X_LANES)
    wrap_shape = (n_lane_tiles, (taps - 1) * V7X_SUBLANES, V7X_LANES)
    vmem = (2 * 2 * d * bn * 4 + 2 * d * bn * 2 + 2 * bm * d * 2 + 2 * bm * bn * 4
            + 2 * (halo + bm) * bn * 4 + 4 * sub * bn * 4 + 2 * slab * do * 6
            + 4 * wrap_shape[0] * wrap_shape[1] * wrap_shape[2])
    return pl.pallas_call(
        kern,
        grid=(nj, ni),
        in_specs=[pl.BlockSpec((bm, d), lambda j, i: (i, 0)),
                  pl.BlockSpec((d, bn), lambda j, i: (0, v_blk0 + j)),
                  pl.BlockSpec((d, bn), lambda j, i: (0, g_blk0 + j)),
                  pl.BlockSpec((taps, bn), lambda j, i: (0, j)),
                  pl.BlockSpec((1, bn), lambda j, i: (0, j)),
                  wo_in],
        out_specs=[pl.BlockSpec((bm, bn), lambda j, i: (i, j)), wo_out],
        out_shape=[jax.ShapeDtypeStruct((m, d_conv), F32),
                   jax.ShapeDtypeStruct((ko, do), BF16)],
        scratch_shapes=[pltpu.VMEM((d, 2 * bn), BF16),
                        pltpu.VMEM(win_shape, F32),
                        pltpu.VMEM(wrap_shape, F32),
                        pltpu.VMEM(win_shape, F32)],
        compiler_params=_params(("arbitrary", "arbitrary"), vmem),
        name="glu_conv_branch",
    )(h, w_in, w_in, conv_w, conv_b.reshape(1, d_conv), w_out)


def _residual_norms(m, x_ref, gpost_ref, gnext_ref, xo_ref, ho_ref, rows=slice(None)):
    x_new = x_ref[rows, :] + _rms(m, gpost_ref[...])
    xo_ref[rows, :] = x_new
    if ho_ref is not None:
        ho_ref[rows, :] = _rms(x_new, gnext_ref[...]).astype(ho_ref.dtype)


def _lagged_steps(i, n_tiles, produce, consume, produce_first=None):
    @pl.when(i == 0)
    def _():
        (produce_first or produce)()

    @pl.when(jnp.logical_and(i > 0, i < n_tiles))
    def _():
        consume()
        produce()

    @pl.when(i == n_tiles)
    def _():
        consume()


def _out_proj_l0_kernel(yp_ref, c_ref, x_ref, w_ref, lng_ref, lnb_ref, gpost_ref, gnext_ref,
                        xo_ref, ho_ref, *, d_pool, bm, sub):
    for s0 in range(0, bm, sub):
        rows = slice(s0, s0 + sub)
        c = c_ref[rows, :]
        mu = jnp.mean(c, axis=-1, keepdims=True)
        cc = c - mu
        var = jnp.mean(cc * cc, axis=-1, keepdims=True)
        ln = cc * lax.rsqrt(var + EPS) * lng_ref[...] + lnb_ref[...]
        y_conv = (ln * jax.nn.sigmoid(ln)).astype(BF16)
        m = jnp.dot(yp_ref[rows, :], w_ref[0:d_pool, :], preferred_element_type=F32)
        m = m + jnp.dot(y_conv, w_ref[d_pool:, :], preferred_element_type=F32)
        _residual_norms(m, x_ref, gpost_ref, gnext_ref, xo_ref, ho_ref, rows)


def _out_proj_l1_kernel(y_ref, x_ref, w_ref, gpost_ref, gnext_ref, xo_ref, ho_ref, *, bm, sub):
    for s0 in range(0, bm, sub):
        rows = slice(s0, s0 + sub)
        m = jnp.dot(y_ref[rows, :], w_ref[...], preferred_element_type=F32)
        _residual_norms(m, x_ref, gpost_ref, gnext_ref, xo_ref, ho_ref, rows)


def _out_proj(ys, x, w_bf, g_post, g_next, ln=None, *, bm, sub=256):
    assert bm % sub == 0
    m, d = x.shape
    k = w_bf.shape[0]
    row = lambda i: (i, 0)
    fixed = lambda i: (0, 0)
    vec = lambda a: a.reshape(1, -1)
    if ln is None:
        (y,) = ys
        kern = functools.partial(_out_proj_l1_kernel, bm=bm, sub=sub)
        ins = [y, x, w_bf, vec(g_post), vec(g_next)]
        in_specs = [pl.BlockSpec((bm, k), row), pl.BlockSpec((bm, d), row),
                    _const_spec((k, d), fixed), _const_spec((1, d), fixed), _const_spec((1, d), fixed)]
        tile_bytes = bm * k * 2
    else:
        y_pool, c = ys
        d_pool, d_conv = y_pool.shape[1], c.shape[1]
        kern = functools.partial(_out_proj_l0_kernel, d_pool=d_pool, bm=bm, sub=sub)
        ins = [y_pool, c, x, w_bf, vec(ln[0]), vec(ln[1]), vec(g_post), vec(g_next)]
        in_specs = [pl.BlockSpec((bm, d_pool), row), pl.BlockSpec((bm, d_conv), row),
                    pl.BlockSpec((bm, d), row), _const_spec((k, d), fixed),
                    _const_spec((1, d_conv), fixed), _const_spec((1, d_conv), fixed),
                    _const_spec((1, d), fixed), _const_spec((1, d), fixed)]
        tile_bytes = bm * d_pool * 2 + bm * d_conv * 4
    vmem = k * d * 2 + 2 * tile_bytes + 2 * bm * d * (4 + 4 + 2) + 5 * bm * d * 4
    return pl.pallas_call(
        kern,
        grid=(m // bm,),
        in_specs=in_specs,
        out_specs=[pl.BlockSpec((bm, d), row), pl.BlockSpec((bm, d), row)],
        out_shape=[jax.ShapeDtypeStruct((m, d), F32), jax.ShapeDtypeStruct((m, d), BF16)],
        compiler_params=_params(("arbitrary",), vmem),
        name="out_proj_l0" if ln is not None else "out_proj_l1",
    )(*ins)


def _ffn_up_kernel(h_ref, w_ref, w2_ref, o_ref, w2bf_ref, wbf, *, bm, sub):
    @pl.when(pl.program_id(1) == 0)
    def _():
        _cast_weight(wbf, w_ref)

    w2bf_ref[...] = w2_ref[...].astype(BF16)
    for s0 in range(0, bm, sub):
        rows = slice(s0, s0 + sub)
        a = jnp.maximum(jnp.dot(h_ref[rows, :], wbf[...], preferred_element_type=F32), 0.0)
        o_ref[rows, :] = (a * a).astype(o_ref.dtype)


def _ffn_up(h, w1, w2, layer, *, bm, bn):
    m, d = h.shape
    f = w1.shape[2]
    nj, ni = f // bn, m // bm
    slab = f // (nj * ni)
    slab_index = lambda j, i: (j * ni + i, 0)
    sub = min(bm, 512)
    vmem = (2 * d * bn * 4 + d * bn * 2 + 2 * bm * d * 2 + 2 * bm * bn * 2 + 3 * sub * bn * 4
            + 2 * slab * d * 6)
    return pl.pallas_call(
        functools.partial(_ffn_up_kernel, bm=bm, sub=sub),
        grid=(nj, ni),
        in_specs=[pl.BlockSpec((bm, d), lambda j, i: (i, 0)),
                  pl.BlockSpec((None, d, bn), lambda j, i: (layer, 0, j)),
                  pl.BlockSpec((None, slab, d), lambda j, i: (layer, j * ni + i, 0))],
        out_specs=[pl.BlockSpec((bm, bn), lambda j, i: (i, j)),
                   pl.BlockSpec((slab, d), slab_index)],
        out_shape=[jax.ShapeDtypeStruct((m, f), BF16),
                   jax.ShapeDtypeStruct((f, d), BF16)],
        scratch_shapes=[pltpu.VMEM((d, bn), BF16)],
        compiler_params=_params(("arbitrary", "arbitrary"), vmem),
        name="ffn_up",
    )(h, w1, w2)


def _ffn_down_kernel(a_ref, w_hbm, x_ref, gpost_ref, gnext_ref, xo_ref, *rest, with_next, n_tiles,
                     n_chunks):
    ho_ref = rest[0] if with_next else None
    m_buf, w_ref, sems = rest[-3:]
    kc = w_ref.shape[0] // n_chunks

    def chunk_copy(c):
        rows = pl.ds(c * kc, kc)
        return pltpu.make_async_copy(w_hbm.at[rows, :], w_ref.at[rows, :], sems.at[c])

    def first_matmul():
        for c in range(n_chunks):
            chunk_copy(c).start()
        for c in range(n_chunks):
            chunk_copy(c).wait()
            part = jnp.dot(a_ref[:, c * kc:(c + 1) * kc], w_ref[c * kc:(c + 1) * kc, :],
                           preferred_element_type=F32)
            if c == 0:
                m_buf[...] = part
            else:
                m_buf[...] += part

    def matmul():
        m_buf[...] = jnp.dot(a_ref[...], w_ref[...], preferred_element_type=F32)

    def norms():
        _residual_norms(m_buf[...], x_ref, gpost_ref, gnext_ref, xo_ref, ho_ref)

    _lagged_steps(pl.program_id(0), n_tiles, matmul, norms, produce_first=first_matmul)


def _ffn_down(a, w2_bf, x, g_post, g_next, *, bm):
    m, f = a.shape
    d = w2_bf.shape[1]
    n_tiles = m // bm
    with_next = g_next is not None
    if not with_next:
        g_next = g_post
    ahead = lambda i: (jnp.minimum(i, n_tiles - 1), 0)
    behind = lambda i: (jnp.maximum(i - 1, 0), 0)
    fixed = lambda i: (0, 0)
    out_specs = [pl.BlockSpec((bm, d), behind)]
    out_shape = [jax.ShapeDtypeStruct((m, d), F32)]
    if with_next:
        out_specs.append(pl.BlockSpec((bm, d), behind))
        out_shape.append(jax.ShapeDtypeStruct((m, d), BF16))
    vmem = f * d * 2 + 2 * bm * f * 2 + 2 * bm * d * (4 + 4 + 2) + 4 * bm * d * 4
    n_chunks = 8
    assert f % n_chunks == 0
    outs = pl.pallas_call(
        functools.partial(_ffn_down_kernel, with_next=with_next, n_tiles=n_tiles,
                          n_chunks=n_chunks),
        grid=(n_tiles + 1,),
        in_specs=[pl.BlockSpec((bm, f), ahead),
                  pl.BlockSpec(memory_space=pl.ANY),
                  pl.BlockSpec((bm, d), behind),
                  _const_spec((1, d), fixed), _const_spec((1, d), fixed)],
        out_specs=out_specs,
        out_shape=out_shape,
        scratch_shapes=[pltpu.VMEM((bm, d), F32),
                        pltpu.VMEM((f, d), BF16),
                        pltpu.SemaphoreType.DMA((n_chunks,))],
        compiler_params=_params(("arbitrary",), vmem),
        name="ffn_down",
    )(a, w2_bf, x, g_post.reshape(1, d), g_next.reshape(1, d))
    return outs if with_next else (outs[0], None)


def _gated_conv_kernel(h_ref, wb_ref, wc_ref, wu_ref, cw_ref, wo_ref, o_ref, wobf_ref,
                       wall, pbuf, *, bm, bn, seq, halo, taps, sub):
    i = pl.program_id(1)

    @pl.when(i == 0)
    def _():
        _cast_weight(wall, wc_ref, col0=0)
        _cast_weight(wall, wu_ref, col0=bn)
        _cast_weight(wall, wb_ref, col0=2 * bn)

    wobf_ref[...] = wo_ref[...].astype(BF16)
    _load_history(pbuf, i, bm=bm, seq=seq, halo=halo)
    cw = cw_ref[...]
    for s0 in range(0, bm, sub):
        z = jnp.dot(h_ref[s0:s0 + sub, :], wall[...], preferred_element_type=F32)
        pbuf[halo + s0:halo + s0 + sub, :] = z[:, 0:bn] * z[:, bn:2 * bn]
        conv = cw[taps - 1:taps, :] * pbuf[halo + s0:halo + s0 + sub, :]
        for k in range(taps - 1):
            start = halo + s0 - (taps - 1) + k
            conv = conv + cw[k:k + 1, :] * pbuf[start:start + sub, :]
        o_ref[s0:s0 + sub, :] = (z[:, 2 * bn:] * conv).astype(o_ref.dtype)


def _gated_conv_branch(h, w_in, conv_w, w_out, *, bm, bn, seq):
    m, d = h.shape
    taps, d_short = conv_w.shape
    halo = _halo_rows(taps - 1)
    nj, ni = d_short // bn, m // bm
    ko, do = w_out.shape
    slab = ko // (nj * ni)
    wo_in, wo_out = _cast_slab_specs(slab, do, ni)
    assert seq % bm == 0 and m % seq == 0
    assert ko % (nj * ni) == 0 and d_short % bn == 0
    kern = functools.partial(_gated_conv_kernel, bm=bm, bn=bn, seq=seq, halo=halo, taps=taps,
                             sub=min(bm, 512))
    vmem = (3 * 2 * d * bn * 4 + 3 * d * bn * 2 + 2 * bm * d * 2 + 2 * bm * bn * 2
            + (halo + bm) * bn * 4 + 5 * bm * bn * 4 + 2 * slab * do * 6)
    return pl.pallas_call(
        kern,
        grid=(nj, ni),
        in_specs=[pl.BlockSpec((bm, d), lambda j, i: (i, 0)),
                  pl.BlockSpec((d, bn), lambda j, i: (0, j)),
                  pl.BlockSpec((d, bn), lambda j, i: (0, nj + j)),
                  pl.BlockSpec((d, bn), lambda j, i: (0, 2 * nj + j)),
                  pl.BlockSpec((taps, bn), lambda j, i: (0, j)),
                  wo_in],
        out_specs=[pl.BlockSpec((bm, bn), lambda j, i: (i, j)), wo_out],
        out_shape=[jax.ShapeDtypeStruct((m, d_short), BF16),
                   jax.ShapeDtypeStruct((ko, do), BF16)],
        scratch_shapes=[pltpu.VMEM((d, 3 * bn), BF16), pltpu.VMEM((halo + bm, bn), F32)],
        compiler_params=_params(("arbitrary", "arbitrary"), vmem),
        name="gated_conv_branch",
    )(h, w_in, w_in, w_in, conv_w, w_out)


def kernel(x, mix_pre_g, mix_post_g, ffn_pre_g, ffn_post_g, ab_w_in, pool_w, pool_scale,
           conv_w, conv_b, conv_ln_g, conv_ln_b, ab_w_out, sc_w_in, sc_conv_w, sc_w_out,
           ffn_w1, ffn_w2):
    batch, seq, d = x.shape
    depth = mix_pre_g.shape[0]
    d_pool = pool_scale.shape[1]
    xs = x.reshape(batch * seq, d)
    h = None
    for layer in range(depth):
        i = layer // 2
        if layer % 2 == 0:
            y_pool, h = _pool_branch(xs, mix_pre_g[layer], ab_w_in[i], pool_w[i], pool_scale[i],
                                     bm=512, seq=seq)
            c, w_out_bf = _glu_conv_branch(h, ab_w_in[i], conv_w[i], conv_b[i], ab_w_out[i],
                                           d_pool=d_pool, bm=1024, bn=512, seq=seq)
            xs, h = _out_proj((y_pool, c), xs, w_out_bf, mix_post_g[layer], ffn_pre_g[layer],
                              ln=(conv_ln_g[i], conv_ln_b[i]), bm=512)
        else:
            y, w_out_bf = _gated_conv_branch(h, sc_w_in[i], sc_conv_w[i], sc_w_out[i],
                                             bm=1024, bn=512, seq=seq)
            xs, h = _out_proj((y,), xs, w_out_bf, mix_post_g[layer], ffn_pre_g[layer], bm=512)
        a, w2_bf = _ffn_up(h, ffn_w1, ffn_w2, layer, bm=2048, bn=1024)
        g_next = mix_pre_g[layer + 1] if layer + 1 < depth and (layer + 1) % 2 == 1 else None
        xs, h = _ffn_down(a, w2_bf, xs, ffn_post_g[layer], g_next, bm=256)
    return xs.reshape(batch, seq, d)
```

```python
import functools

import jax
import jax.numpy as jnp
from jax import lax
from jax.experimental import pallas as pl
from jax.experimental.pallas import tpu as pltpu

EPS = 1e-6
POOL_WINDOWS = (2, 4, 8, 16)

V7X_SUBLANES = 8
V7X_LANES = 128
V7X_VMEM_LIMIT_CAP = 60000 * 1024

F32 = jnp.float32
BF16 = jnp.bfloat16


def _halo_rows(n):
    return -(-n // V7X_SUBLANES) * V7X_SUBLANES


def _params(semantics, vmem_bytes):
    assert vmem_bytes <= 1.35 * V7X_VMEM_LIMIT_CAP
    return pltpu.CompilerParams(
        dimension_semantics=semantics,
        vmem_limit_bytes=V7X_VMEM_LIMIT_CAP)


def _const_spec(shape, index_map):
    return pl.BlockSpec(shape, index_map, pipeline_mode=pl.Buffered(1))


def _cast_weight(dst, src, *, col0=0, rows=256):
    n = src.shape[0] // rows
    cols = slice(col0, col0 + src.shape[1])

    def body(c, carry):
        r = pl.multiple_of(c * rows, rows)
        dst[pl.ds(r, rows), cols] = src[pl.ds(r, rows), :].astype(BF16)
        return carry

    lax.fori_loop(0, n, body, 0)


def _rms(x, g):
    ms = jnp.mean(x * x, axis=-1, keepdims=True)
    return x * lax.rsqrt(ms + EPS) * g


def _load_history(buf, i, *, bm, seq, halo):
    first = lax.rem(i * bm, seq) == 0

    @pl.when(first)
    def _():
        buf[..., 0:halo, :] = jnp.zeros(buf.shape[:-2] + (halo, buf.shape[-1]), buf.dtype)

    @pl.when(jnp.logical_not(first))
    def _():
        buf[..., 0:halo, :] = buf[..., bm:bm + halo, :]


def _pool_kernel(x_ref, g_ref, w_ref, pw_ref, ps_ref, o_ref, h_ref, wbf, pwbf, ubuf,
                 *, bm, seq, halo, group, sub):
    i = pl.program_id(0)

    @pl.when(i == 0)
    def _():
        _cast_weight(wbf, w_ref)
        pwbf[...] = pw_ref[...].astype(BF16)

    _load_history(ubuf, i, bm=bm, seq=seq, halo=halo)
    for s0 in range(0, bm, sub):
        rows = slice(s0, s0 + sub)
        h = _rms(x_ref[rows, :], g_ref[...]).astype(BF16)
        h_ref[rows, :] = h
        ubuf[halo + s0:halo + s0 + sub, :] = jnp.dot(h, wbf[...], preferred_element_type=F32)

        pos = lax.rem(i * bm, seq) + s0 + lax.broadcasted_iota(jnp.int32, (sub, 1), 0)
        for g, w in enumerate(POOL_WINDOWS):
            cols = slice(g * group, (g + 1) * group)
            s = ubuf[s0:s0 + halo + sub, cols]
            span = 1
            while span < w:
                s = s + pltpu.roll(s, span, 0)
                span *= 2
            u = ubuf[halo + s0:halo + s0 + sub, cols]
            cnt = jnp.minimum(pos + 1, w).astype(F32)
            pooled = s[halo:, :] / cnt - u
            mixed = jnp.dot(pooled.astype(BF16), pwbf[g], preferred_element_type=F32)
            o_ref[rows, cols] = (mixed * ps_ref[:, cols]).astype(o_ref.dtype)


def _pool_branch(x, g_pre, w_in, pool_w, pool_scale, *, bm, seq):
    m, d = x.shape
    n_groups, group, _ = pool_w.shape
    d_pool = n_groups * group
    halo = _halo_rows(max(POOL_WINDOWS) - 1)
    assert all(w & (w - 1) == 0 for w in POOL_WINDOWS)
    assert seq % bm == 0 and m % seq == 0
    kern = functools.partial(_pool_kernel, bm=bm, seq=seq, halo=halo, group=group,
                             sub=min(bm, 256))
    vmem = (d * d_pool * (4 + 2) + 2 * bm * d * (4 + 2) + 2 * bm * d_pool * 2
            + (halo + bm) * d_pool * 4 + 3 * bm * d_pool * 4 + pool_w.size * 6 + 3 * bm * d * 4)
    return pl.pallas_call(
        kern,
        grid=(m // bm,),
        in_specs=[pl.BlockSpec((bm, d), lambda i: (i, 0)),
                  _const_spec((1, d), lambda i: (0, 0)),
                  _const_spec((d, d_pool), lambda i: (0, 0)),
                  _const_spec(pool_w.shape, lambda i: (0, 0, 0)),
                  _const_spec((1, d_pool), lambda i: (0, 0))],
        out_specs=[pl.BlockSpec((bm, d_pool), lambda i: (i, 0)),
                   pl.BlockSpec((bm, d), lambda i: (i, 0))],
        out_shape=[jax.ShapeDtypeStruct((m, d_pool), BF16),
                   jax.ShapeDtypeStruct((m, d), BF16)],
        scratch_shapes=[pltpu.VMEM((d, d_pool), BF16),
                        pltpu.VMEM(pool_w.shape, BF16),
                        pltpu.VMEM((halo + bm, d_pool), F32)],
        compiler_params=_params(("arbitrary",), vmem),
        name="pool_branch",
    )(x, g_pre.reshape(1, d), w_in, pool_w, pool_scale.reshape(1, d_pool))


def _cast_slab_specs(rows, d, ni):
    index = lambda j, i: (j * ni + i, 0)
    return pl.BlockSpec((rows, d), index), pl.BlockSpec((rows, d), index)


def _glu_conv_kernel(h_ref, wv_ref, wg_ref, cw_ref, cb_ref, wo_ref, o_ref, wobf_ref,
                     wall, gbuf, wrap, obuf, *, bm, bn, seq, halo, taps, sub):
    i = pl.program_id(1)

    @pl.when(i == 0)
    def _():
        _cast_weight(wall, wv_ref, col0=0)
        _cast_weight(wall, wg_ref, col0=bn)

    wobf_ref[...] = wo_ref[...].astype(BF16)
    _load_history(gbuf, i, bm=bm, seq=seq, halo=halo)

    n_lane_tiles = bn // V7X_LANES
    for s0 in range(0, bm, sub):
        z = jnp.dot(h_ref[s0:s0 + sub, :], wall[...], preferred_element_type=F32)
        g = z[:, 0:bn] * jax.nn.sigmoid(z[:, bn:])
        for c in range(n_lane_tiles):
            gbuf[c, halo + s0:halo + s0 + sub, :] = g[:, c * V7X_LANES:(c + 1) * V7X_LANES]

    span = (halo + bm) // V7X_SUBLANES
    first_wrapped = span - (taps - 1)
    for c in range(n_lane_tiles):
        cs = slice(c * V7X_LANES, (c + 1) * V7X_LANES)
        for u in range(first_wrapped, span):
            k = u - first_wrapped
            wrap[c, k * V7X_SUBLANES:(k + 1) * V7X_SUBLANES, :] = pltpu.roll(
                gbuf[c, pl.ds(u, V7X_SUBLANES, stride=span), :], 1, 0)
        for v0 in range(span):
            acc = jnp.broadcast_to(cb_ref[:, cs], (V7X_SUBLANES, V7X_LANES))
            for s in range(taps):
                if v0 >= s:
                    src = gbuf[c, pl.ds(v0 - s, V7X_SUBLANES, stride=span), :]
                else:
                    k = v0 - s + taps - 1
                    src = wrap[c, k * V7X_SUBLANES:(k + 1) * V7X_SUBLANES, :]
                acc = acc + cw_ref[taps - 1 - s:taps - s, cs] * src
            obuf[c, pl.ds(v0, V7X_SUBLANES, stride=span), :] = acc
        o_ref[:, cs] = obuf[c, halo:, :]


def _glu_conv_branch(h, w_in, conv_w, conv_b, w_out, *, d_pool, bm, bn, seq):
    m, d = h.shape
    taps, d_conv = conv_w.shape
    halo = _halo_rows(taps - 1)
    nj, ni = d_conv // bn, m // bm
    v_blk0 = d_pool // bn
    g_blk0 = (d_pool + d_conv) // bn
    ko, do = w_out.shape
    slab = ko // (nj * ni)
    wo_in, wo_out = _cast_slab_specs(slab, do, ni)
    sub = 256
    assert bm % sub == 0 and bn % V7X_LANES == 0 and halo >= taps - 1
    assert seq % bm == 0 and m % seq == 0
    assert ko % (nj * ni) == 0 and d_pool % bn == 0 and d_conv % bn == 0
    kern = functools.partial(_glu_conv_kernel, bm=bm, bn=bn, seq=seq, halo=halo, taps=taps, sub=sub)
    n_lane_tiles = bn // V7X_LANES
    win_shape = (n_lane_tiles, halo + bm, V7X_LANES)
    wrap_shape = (n_lane_tiles, (taps - 1) * V7X_SUBLANES, V7X_LANES)
    vmem = (2 * 2 * d * bn * 4 + 2 * d * bn * 2 + 2 * bm * d * 2 + 2 * bm * bn * 4
            + 2 * (halo + bm) * bn * 4 + 4 * sub * bn * 4 + 2 * slab * do * 6
            + 4 * wrap_shape[0] * wrap_shape[1] * wrap_shape[2])
    return pl.pallas_call(
        kern,
        grid=(nj, ni),
        in_specs=[pl.BlockSpec((bm, d), lambda j, i: (i, 0)),
                  pl.BlockSpec((d, bn), lambda j, i: (0, v_blk0 + j)),
                  pl.BlockSpec((d, bn), lambda j, i: (0, g_blk0 + j)),
                  pl.BlockSpec((taps, bn), lambda j, i: (0, j)),
                  pl.BlockSpec((1, bn), lambda j, i: (0, j)),
                  wo_in],
        out_specs=[pl.BlockSpec((bm, bn), lambda j, i: (i, j)), wo_out],
        out_shape=[jax.ShapeDtypeStruct((m, d_conv), F32),
                   jax.ShapeDtypeStruct((ko, do), BF16)],
        scratch_shapes=[pltpu.VMEM((d, 2 * bn), BF16),
                        pltpu.VMEM(win_shape, F32),
                        pltpu.VMEM(wrap_shape, F32),
                        pltpu.VMEM(win_shape, F32)],
        compiler_params=_params(("arbitrary", "arbitrary"), vmem),
        name="glu_conv_branch",
    )(h, w_in, w_in, conv_w, conv_b.reshape(1, d_conv), w_out)


def _residual_norms(m, x_ref, gpost_ref, gnext_ref, xo_ref, ho_ref, rows=slice(None)):
    x_new = x_ref[rows, :] + _rms(m, gpost_ref[...])
    xo_ref[rows, :] = x_new
    if ho_ref is not None:
        ho_ref[rows, :] = _rms(x_new, gnext_ref[...]).astype(ho_ref.dtype)


def _lagged_steps(i, n_tiles, produce, consume, produce_first=None):
    @pl.when(i == 0)
    def _():
        (produce_first or produce)()

    @pl.when(jnp.logical_and(i > 0, i < n_tiles))
    def _():
        consume()
        produce()

    @pl.when(i == n_tiles)
    def _():
        consume()


def _out_proj_l0_kernel(yp_ref, c_ref, x_ref, w_ref, lng_ref, lnb_ref, gpost_ref, gnext_ref,
                        xo_ref, ho_ref, *, d_pool, bm, sub):
    for s0 in range(0, bm, sub):
        rows = slice(s0, s0 + sub)
        c = c_ref[rows, :]
        mu = jnp.mean(c, axis=-1, keepdims=True)
        cc = c - mu
        var = jnp.mean(cc * cc, axis=-1, keepdims=True)
        ln = cc * lax.rsqrt(var + EPS) * lng_ref[...] + lnb_ref[...]
        y_conv = (ln * jax.nn.sigmoid(ln)).astype(BF16)
        m = jnp.dot(yp_ref[rows, :], w_ref[0:d_pool, :], preferred_element_type=F32)
        m = m + jnp.dot(y_conv, w_ref[d_pool:, :], preferred_element_type=F32)
        _residual_norms(m, x_ref, gpost_ref, gnext_ref, xo_ref, ho_ref, rows)


def _out_proj_l1_kernel(y_ref, x_ref, w_ref, gpost_ref, gnext_ref, xo_ref, ho_ref, *, bm, sub):
    for s0 in range(0, bm, sub):
        rows = slice(s0, s0 + sub)
        m = jnp.dot(y_ref[rows, :], w_ref[...], preferred_element_type=F32)
        _residual_norms(m, x_ref, gpost_ref, gnext_ref, xo_ref, ho_ref, rows)


def _out_proj(ys, x, w_bf, g_post, g_next, ln=None, *, bm, sub=256):
    assert bm % sub == 0
    m, d = x.shape
    k = w_bf.shape[0]
    row = lambda i: (i, 0)
    fixed = lambda i: (0, 0)
    vec = lambda a: a.reshape(1, -1)
    if ln is None:
        (y,) = ys
        kern = functools.partial(_out_proj_l1_kernel, bm=bm, sub=sub)
        ins = [y, x, w_bf, vec(g_post), vec(g_next)]
        in_specs = [pl.BlockSpec((bm, k), row), pl.BlockSpec((bm, d), row),
                    _const_spec((k, d), fixed), _const_spec((1, d), fixed), _const_spec((1, d), fixed)]
        tile_bytes = bm * k * 2
    else:
        y_pool, c = ys
        d_pool, d_conv = y_pool.shape[1], c.shape[1]
        kern = functools.partial(_out_proj_l0_kernel, d_pool=d_pool, bm=bm, sub=sub)
        ins = [y_pool, c, x, w_bf, vec(ln[0]), vec(ln[1]), vec(g_post), vec(g_next)]
        in_specs = [pl.BlockSpec((bm, d_pool), row), pl.BlockSpec((bm, d_conv), row),
                    pl.BlockSpec((bm, d), row), _const_spec((k, d), fixed),
                    _const_spec((1, d_conv), fixed), _const_spec((1, d_conv), fixed),
                    _const_spec((1, d), fixed), _const_spec((1, d), fixed)]
        tile_bytes = bm * d_pool * 2 + bm * d_conv * 4
    vmem = k * d * 2 + 2 * tile_bytes + 2 * bm * d * (4 + 4 + 2) + 5 * bm * d * 4
    return pl.pallas_call(
        kern,
        grid=(m // bm,),
        in_specs=in_specs,
        out_specs=[pl.BlockSpec((bm, d), row), pl.BlockSpec((bm, d), row)],
        out_shape=[jax.ShapeDtypeStruct((m, d), F32), jax.ShapeDtypeStruct((m, d), BF16)],
        compiler_params=_params(("arbitrary",), vmem),
        name="out_proj_l0" if ln is not None else "out_proj_l1",
    )(*ins)


def _ffn_up_kernel(h_ref, w_ref, w2_ref, o_ref, w2bf_ref, wbf, *, bm, sub):
    @pl.when(pl.program_id(1) == 0)
    def _():
        _cast_weight(wbf, w_ref)

    w2bf_ref[...] = w2_ref[...].astype(BF16)
    for s0 in range(0, bm, sub):
        rows = slice(s0, s0 + sub)
        a = jnp.maximum(jnp.dot(h_ref[rows, :], wbf[...], preferred_element_type=F32), 0.0)
        o_ref[rows, :] = (a * a).astype(o_ref.dtype)


def _ffn_up(h, w1, w2, layer, *, bm, bn):
    m, d = h.shape
    f = w1.shape[2]
    nj, ni = f // bn, m // bm
    slab = f // (nj * ni)
    slab_index = lambda j, i: (j * ni + i, 0)
    sub = min(bm, 512)
    vmem = (2 * d * bn * 4 + d * bn * 2 + 2 * bm * d * 2 + 2 * bm * bn * 2 + 3 * sub * bn * 4
            + 2 * slab * d * 6)
    return pl.pallas_call(
        functools.partial(_ffn_up_kernel, bm=bm, sub=sub),
        grid=(nj, ni),
        in_specs=[pl.BlockSpec((bm, d), lambda j, i: (i, 0)),
                  pl.BlockSpec((None, d, bn), lambda j, i: (layer, 0, j)),
                  pl.BlockSpec((None, slab, d), lambda j, i: (layer, j * ni + i, 0))],
        out_specs=[pl.BlockSpec((bm, bn), lambda j, i: (i, j)),
                   pl.BlockSpec((slab, d), slab_index)],
        out_shape=[jax.ShapeDtypeStruct((m, f), BF16),
                   jax.ShapeDtypeStruct((f, d), BF16)],
        scratch_shapes=[pltpu.VMEM((d, bn), BF16)],
        compiler_params=_params(("arbitrary", "arbitrary"), vmem),
        name="ffn_up",
    )(h, w1, w2)


def _ffn_down_kernel(a_ref, w_hbm, x_ref, gpost_ref, gnext_ref, xo_ref, *rest, with_next, n_tiles,
                     n_chunks):
    ho_ref = rest[0] if with_next else None
    m_buf, w_ref, sems = rest[-3:]
    kc = w_ref.shape[0] // n_chunks

    def chunk_copy(c):
        rows = pl.ds(c * kc, kc)
        return pltpu.make_async_copy(w_hbm.at[rows, :], w_ref.at[rows, :], sems.at[c])

    def first_matmul():
        for c in range(n_chunks):
            chunk_copy(c).start()
        for c in range(n_chunks):
            chunk_copy(c).wait()
            part = jnp.dot(a_ref[:, c * kc:(c + 1) * kc], w_ref[c * kc:(c + 1) * kc, :],
                           preferred_element_type=F32)
            if c == 0:
                m_buf[...] = part
            else:
                m_buf[...] += part

    def matmul():
        m_buf[...] = jnp.dot(a_ref[...], w_ref[...], preferred_element_type=F32)

    def norms():
        _residual_norms(m_buf[...], x_ref, gpost_ref, gnext_ref, xo_ref, ho_ref)

    _lagged_steps(pl.program_id(0), n_tiles, matmul, norms, produce_first=first_matmul)


def _ffn_down(a, w2_bf, x, g_post, g_next, *, bm):
    m, f = a.shape
    d = w2_bf.shape[1]
    n_tiles = m // bm
    with_next = g_next is not None
    if not with_next:
        g_next = g_post
    ahead = lambda i: (jnp.minimum(i, n_tiles - 1), 0)
    behind = lambda i: (jnp.maximum(i - 1, 0), 0)
    fixed = lambda i: (0, 0)
    out_specs = [pl.BlockSpec((bm, d), behind)]
    out_shape = [jax.ShapeDtypeStruct((m, d), F32)]
    if with_next:
        out_specs.append(pl.BlockSpec((bm, d), behind))
        out_shape.append(jax.ShapeDtypeStruct((m, d), BF16))
    vmem = f * d * 2 + 2 * bm * f * 2 + 2 * bm * d * (4 + 4 + 2) + 4 * bm * d * 4
    n_chunks = 8
    assert f % n_chunks == 0
    outs = pl.pallas_call(
        functools.partial(_ffn_down_kernel, with_next=with_next, n_tiles=n_tiles,
                          n_chunks=n_chunks),
        grid=(n_tiles + 1,),
        in_specs=[pl.BlockSpec((bm, f), ahead),
                  pl.BlockSpec(memory_space=pl.ANY),
                  pl.BlockSpec((bm, d), behind),
                  _const_spec((1, d), fixed), _const_spec((1, d), fixed)],
        out_specs=out_specs,
        out_shape=out_shape,
        scratch_shapes=[pltpu.VMEM((bm, d), F32),
                        pltpu.VMEM((f, d), BF16),
                        pltpu.SemaphoreType.DMA((n_chunks,))],
        compiler_params=_params(("arbitrary",), vmem),
        name="ffn_down",
    )(a, w2_bf, x, g_post.reshape(1, d), g_next.reshape(1, d))
    return outs if with_next else (outs[0], None)


def _gated_conv_kernel(h_ref, wb_ref, wc_ref, wu_ref, cw_ref, wo_ref, o_ref, wobf_ref,
                       wall, pbuf, *, bm, bn, seq, halo, taps, sub):
    i = pl.program_id(1)

    @pl.when(i == 0)
    def _():
        _cast_weight(wall, wc_ref, col0=0)
        _cast_weight(wall, wu_ref, col0=bn)
        _cast_weight(wall, wb_ref, col0=2 * bn)

    wobf_ref[...] = wo_ref[...].astype(BF16)
    _load_history(pbuf, i, bm=bm, seq=seq, halo=halo)
    cw = cw_ref[...]
    for s0 in range(0, bm, sub):
        z = jnp.dot(h_ref[s0:s0 + sub, :], wall[...], preferred_element_type=F32)
        pbuf[halo + s0:halo + s0 + sub, :] = z[:, 0:bn] * z[:, bn:2 * bn]
        conv = cw[taps - 1:taps, :] * pbuf[halo + s0:halo + s0 + sub, :]
        for k in range(taps - 1):
            start = halo + s0 - (taps - 1) + k
            conv = conv + cw[k:k + 1, :] * pbuf[start:start + sub, :]
        o_ref[s0:s0 + sub, :] = (z[:, 2 * bn:] * conv).astype(o_ref.dtype)


def _gated_conv_branch(h, w_in, conv_w, w_out, *, bm, bn, seq):
    m, d = h.shape
    taps, d_short = conv_w.shape
    halo = _halo_rows(taps - 1)
    nj, ni = d_short // bn, m // bm
    ko, do = w_out.shape
    slab = ko // (nj * ni)
    wo_in, wo_out = _cast_slab_specs(slab, do, ni)
    assert seq % bm == 0 and m % seq == 0
    assert ko % (nj * ni) == 0 and d_short % bn == 0
    kern = functools.partial(_gated_conv_kernel, bm=bm, bn=bn, seq=seq, halo=halo, taps=taps,
                             sub=min(bm, 512))
    vmem = (3 * 2 * d * bn * 4 + 3 * d * bn * 2 + 2 * bm * d * 2 + 2 * bm * bn * 2
            + (halo + bm) * bn * 4 + 5 * bm * bn * 4 + 2 * slab * do * 6)
    return pl.pallas_call(
        kern,
        grid=(nj, ni),
        in_specs=[pl.BlockSpec((bm, d), lambda j, i: (i, 0)),
                  pl.BlockSpec((d, bn), lambda j, i: (0, j)),
                  pl.BlockSpec((d, bn), lambda j, i: (0, nj + j)),
                  pl.BlockSpec((d, bn), lambda j, i: (0, 2 * nj + j)),
                  pl.BlockSpec((taps, bn), lambda j, i: (0, j)),
                  wo_in],
        out_specs=[pl.BlockSpec((bm, bn), lambda j, i: (i, j)), wo_out],
        out_shape=[jax.ShapeDtypeStruct((m, d_short), BF16),
                   jax.ShapeDtypeStruct((ko, do), BF16)],
        scratch_shapes=[pltpu.VMEM((d, 3 * bn), BF16), pltpu.VMEM((halo + bm, bn), F32)],
        compiler_params=_params(("arbitrary", "arbitrary"), vmem),
        name="gated_conv_branch",
    )(h, w_in, w_in, w_in, conv_w, w_out)


def kernel(x, mix_pre_g, mix_post_g, ffn_pre_g, ffn_post_g, ab_w_in, pool_w, pool_scale,
           conv_w, conv_b, conv_ln_g, conv_ln_b, ab_w_out, sc_w_in, sc_conv_w, sc_w_out,
           ffn_w1, ffn_w2):
    batch, seq, d = x.shape
    depth = mix_pre_g.shape[0]
    d_pool = pool_scale.shape[1]
    xs = x.reshape(batch * seq, d)
    h = None
    for layer in range(depth):
        i = layer // 2
        if layer % 2 == 0:
            y_pool, h = _pool_branch(xs, mix_pre_g[layer], ab_w_in[i], pool_w[i], pool_scale[i],
                                     bm=512, seq=seq)
            c, w_out_bf = _glu_conv_branch(h, ab_w_in[i], conv_w[i], conv_b[i], ab_w_out[i],
                                           d_pool=d_pool, bm=1024, bn=512, seq=seq)
            xs, h = _out_proj((y_pool, c), xs, w_out_bf, mix_post_g[layer], ffn_pre_g[layer],
                              ln=(conv_ln_g[i], conv_ln_b[i]), bm=512)
        else:
            y, w_out_bf = _gated_conv_branch(h, sc_w_in[i], sc_conv_w[i], sc_w_out[i],
                                             bm=1024, bn=512, seq=seq)
            xs, h = _out_proj((y,), xs, w_out_bf, mix_post_g[layer], ffn_pre_g[layer], bm=512)
        a, w2_bf = _ffn_up(h, ffn_w1, ffn_w2, layer, bm=2048, bn=1024)
        g_next = mix_pre_g[layer + 1] if layer + 1 < depth and (layer + 1) % 2 == 1 else None
        xs, h = _ffn_down(a, w2_bf, xs, ffn_post_g[layer], g_next, bm=256)
    return xs.reshape(batch, seq, d)
```

```python
import functools

import jax
import jax.numpy as jnp
from jax import lax
from jax.experimental import pallas as pl
from jax.experimental.pallas import tpu as pltpu

EPS = 1e-6
POOL_WINDOWS = (2, 4, 8, 16)

V7X_SUBLANES = 8
V7X_LANES = 128
V7X_VMEM_LIMIT_CAP = 60000 * 1024

F32 = jnp.float32
BF16 = jnp.bfloat16


def _halo_rows(n):
    return -(-n // V7X_SUBLANES) * V7X_SUBLANES


def _params(semantics, vmem_bytes):
    assert vmem_bytes <= 1.35 * V7X_VMEM_LIMIT_CAP
    return pltpu.CompilerParams(
        dimension_semantics=semantics,
        vmem_limit_bytes=V7X_VMEM_LIMIT_CAP)


def _const_spec(shape, index_map):
    return pl.BlockSpec(shape, index_map, pipeline_mode=pl.Buffered(1))


def _cast_weight(dst, src, *, col0=0, rows=256):
    n = src.shape[0] // rows
    cols = slice(col0, col0 + src.shape[1])

    def body(c, carry):
        r = pl.multiple_of(c * rows, rows)
        dst[pl.ds(r, rows), cols] = src[pl.ds(r, rows), :].astype(BF16)
        return carry

    lax.fori_loop(0, n, body, 0)


def _rms(x, g):
    ms = jnp.mean(x * x, axis=-1, keepdims=True)
    return x * lax.rsqrt(ms + EPS) * g


def _load_history(buf, i, *, bm, seq, halo):
    first = lax.rem(i * bm, seq) == 0

    @pl.when(first)
    def _():
        buf[..., 0:halo, :] = jnp.zeros(buf.shape[:-2] + (halo, buf.shape[-1]), buf.dtype)

    @pl.when(jnp.logical_not(first))
    def _():
        buf[..., 0:halo, :] = buf[..., bm:bm + halo, :]


def _pool_kernel(x_ref, g_ref, w_ref, pw_ref, ps_ref, o_ref, h_ref, wbf, pwbf, ubuf,
                 *, bm, seq, halo, group, sub, g_row):
    i = pl.program_id(0)

    @pl.when(i == 0)
    def _():
        _cast_weight(wbf, w_ref)
        pwbf[...] = pw_ref[...].astype(BF16)

    _load_history(ubuf, i, bm=bm, seq=seq, halo=halo)
    for s0 in range(0, bm, sub):
        rows = slice(s0, s0 + sub)
        h = _rms(x_ref[rows, :], g_ref[g_row:g_row + 1, :]).astype(BF16)
        h_ref[rows, :] = h
        ubuf[halo + s0:halo + s0 + sub, :] = jnp.dot(h, wbf[...], preferred_element_type=F32)

        pos = lax.rem(i * bm, seq) + s0 + lax.broadcasted_iota(jnp.int32, (sub, 1), 0)
        for g, w in enumerate(POOL_WINDOWS):
            cols = slice(g * group, (g + 1) * group)
            s = ubuf[s0:s0 + halo + sub, cols]
            span = 1
            while span < w:
                s = s + pltpu.roll(s, span, 0)
                span *= 2
            u = ubuf[halo + s0:halo + s0 + sub, cols]
            cnt = jnp.minimum(pos + 1, w).astype(F32)
            pooled = s[halo:, :] / cnt - u
            mixed = jnp.dot(pooled.astype(BF16), pwbf[g], preferred_element_type=F32)
            o_ref[rows, cols] = (mixed * ps_ref[:, cols]).astype(o_ref.dtype)


def _pool_branch(x, g_pre, w_in, pool_w, pool_scale, *, bm, seq):
    g_all, g_row = g_pre
    m, d = x.shape
    n_groups, group, _ = pool_w.shape
    d_pool = n_groups * group
    halo = _halo_rows(max(POOL_WINDOWS) - 1)
    assert all(w & (w - 1) == 0 for w in POOL_WINDOWS)
    assert seq % bm == 0 and m % seq == 0
    kern = functools.partial(_pool_kernel, bm=bm, seq=seq, halo=halo, group=group,
                             sub=min(bm, 256), g_row=g_row)
    vmem = (d * d_pool * (4 + 2) + 2 * bm * d * (4 + 2) + 2 * bm * d_pool * 2
            + (halo + bm) * d_pool * 4 + 3 * bm * d_pool * 4 + pool_w.size * 6 + 3 * bm * d * 4)
    return pl.pallas_call(
        kern,
        grid=(m // bm,),
        in_specs=[pl.BlockSpec((bm, d), lambda i: (i, 0)),
                  _const_spec(g_all.shape, lambda i: (0, 0)),
                  _const_spec((d, d_pool), lambda i: (0, 0)),
                  _const_spec(pool_w.shape, lambda i: (0, 0, 0)),
                  _const_spec((1, d_pool), lambda i: (0, 0))],
        out_specs=[pl.BlockSpec((bm, d_pool), lambda i: (i, 0)),
                   pl.BlockSpec((bm, d), lambda i: (i, 0))],
        out_shape=[jax.ShapeDtypeStruct((m, d_pool), BF16),
                   jax.ShapeDtypeStruct((m, d), BF16)],
        scratch_shapes=[pltpu.VMEM((d, d_pool), BF16),
                        pltpu.VMEM(pool_w.shape, BF16),
                        pltpu.VMEM((halo + bm, d_pool), F32)],
        compiler_params=_params(("arbitrary",), vmem),
        name="pool_branch",
    )(x, g_all, w_in, pool_w, pool_scale.reshape(1, d_pool))


def _cast_slab_specs(rows, d, ni):
    index = lambda j, i: (j * ni + i, 0)
    return pl.BlockSpec((rows, d), index), pl.BlockSpec((rows, d), index)


def _glu_conv_kernel(h_ref, wv_ref, wg_ref, cw_ref, cb_ref, wo_ref, o_ref, wobf_ref,
                     wall, gbuf, wrap, obuf, *, bm, bn, seq, halo, taps, sub):
    i = pl.program_id(1)

    @pl.when(i == 0)
    def _():
        _cast_weight(wall, wv_ref, col0=0)
        _cast_weight(wall, wg_ref, col0=bn)

    wobf_ref[...] = wo_ref[...].astype(BF16)
    _load_history(gbuf, i, bm=bm, seq=seq, halo=halo)

    n_lane_tiles = bn // V7X_LANES
    for s0 in range(0, bm, sub):
        z = jnp.dot(h_ref[s0:s0 + sub, :], wall[...], preferred_element_type=F32)
        g = z[:, 0:bn] * jax.nn.sigmoid(z[:, bn:])
        for c in range(n_lane_tiles):
            gbuf[c, halo + s0:halo + s0 + sub, :] = g[:, c * V7X_LANES:(c + 1) * V7X_LANES]

    span = (halo + bm) // V7X_SUBLANES
    first_wrapped = span - (taps - 1)
    for c in range(n_lane_tiles):
        cs = slice(c * V7X_LANES, (c + 1) * V7X_LANES)
        for u in range(first_wrapped, span):
            k = u - first_wrapped
            wrap[c, k * V7X_SUBLANES:(k + 1) * V7X_SUBLANES, :] = pltpu.roll(
                gbuf[c, pl.ds(u, V7X_SUBLANES, stride=span), :], 1, 0)
        for v0 in range(span):
            acc = jnp.broadcast_to(cb_ref[:, cs], (V7X_SUBLANES, V7X_LANES))
            for s in range(taps):
                if v0 >= s:
                    src = gbuf[c, pl.ds(v0 - s, V7X_SUBLANES, stride=span), :]
                else:
                    k = v0 - s + taps - 1
                    src = wrap[c, k * V7X_SUBLANES:(k + 1) * V7X_SUBLANES, :]
                acc = acc + cw_ref[taps - 1 - s:taps - s, cs] * src
            obuf[c, pl.ds(v0, V7X_SUBLANES, stride=span), :] = acc
        o_ref[:, cs] = obuf[c, halo:, :]


def _glu_conv_branch(h, w_in, conv_w, conv_b, w_out, *, d_pool, bm, bn, seq):
    conv_w, conv_idx = conv_w
    m, d = h.shape
    _, taps, d_conv = conv_w.shape
    halo = _halo_rows(taps - 1)
    nj, ni = d_conv // bn, m // bm
    v_blk0 = d_pool // bn
    g_blk0 = (d_pool + d_conv) // bn
    ko, do = w_out.shape
    slab = ko // (nj * ni)
    wo_in, wo_out = _cast_slab_specs(slab, do, ni)
    sub = 256
    assert bm % sub == 0 and bn % V7X_LANES == 0 and halo >= taps - 1
    assert seq % bm == 0 and m % seq == 0
    assert ko % (nj * ni) == 0 and d_pool % bn == 0 and d_conv % bn == 0
    kern = functools.partial(_glu_conv_kernel, bm=bm, bn=bn, seq=seq, halo=halo, taps=taps, sub=sub)
    n_lane_tiles = bn // V7X_LANES
    win_shape = (n_lane_tiles, halo + bm, V7X_LANES)
    wrap_shape = (n_lane_tiles, (taps - 1) * V7X_SUBLANES, V7X_LANES)
    vmem = (2 * 2 * d * bn * 4 + 2 * d * bn * 2 + 2 * bm * d * 2 + 2 * bm * bn * 4
            + 2 * (halo + bm) * bn * 4 + 4 * sub * bn * 4 + 2 * slab * do * 6
            + 4 * wrap_shape[0] * wrap_shape[1] * wrap_shape[2])
    return pl.pallas_call(
        kern,
        grid=(nj, ni),
        in_specs=[pl.BlockSpec((bm, d), lambda j, i: (i, 0)),
                  pl.BlockSpec((d, bn), lambda j, i: (0, v_blk0 + j)),
                  pl.BlockSpec((d, bn), lambda j, i: (0, g_blk0 + j)),
                  pl.BlockSpec((None, taps, bn), lambda j, i: (conv_idx, 0, j)),
                  pl.BlockSpec((1, bn), lambda j, i: (0, j)),
                  wo_in],
        out_specs=[pl.BlockSpec((bm, bn), lambda j, i: (i, j)), wo_out],
        out_shape=[jax.ShapeDtypeStruct((m, d_conv), F32),
                   jax.ShapeDtypeStruct((ko, do), BF16)],
        scratch_shapes=[pltpu.VMEM((d, 2 * bn), BF16),
                        pltpu.VMEM(win_shape, F32),
                        pltpu.VMEM(wrap_shape, F32),
                        pltpu.VMEM(win_shape, F32)],
        compiler_params=_params(("arbitrary", "arbitrary"), vmem),
        name="glu_conv_branch",
    )(h, w_in, w_in, conv_w, conv_b.reshape(1, d_conv), w_out)


def _residual_norms(m, x_ref, gpost_ref, gnext_ref, xo_ref, ho_ref, g_rows, rows=slice(None)):
    post, nxt = g_rows
    x_new = x_ref[rows, :] + _rms(m, gpost_ref[post:post + 1, :])
    xo_ref[rows, :] = x_new
    if ho_ref is not None:
        ho_ref[rows, :] = _rms(x_new, gnext_ref[nxt:nxt + 1, :]).astype(ho_ref.dtype)


def _lagged_steps(i, n_tiles, produce, consume, produce_first=None):
    @pl.when(i == 0)
    def _():
        (produce_first or produce)()

    @pl.when(jnp.logical_and(i > 0, i < n_tiles))
    def _():
        consume()
        produce()

    @pl.when(i == n_tiles)
    def _():
        consume()


def _out_proj_l0_kernel(yp_ref, c_ref, x_ref, w_ref, lng_ref, lnb_ref, gpost_ref, gnext_ref,
                        xo_ref, ho_ref, *, d_pool, bm, sub, g_rows):
    for s0 in range(0, bm, sub):
        rows = slice(s0, s0 + sub)
        c = c_ref[rows, :]
        mu = jnp.mean(c, axis=-1, keepdims=True)
        cc = c - mu
        var = jnp.mean(cc * cc, axis=-1, keepdims=True)
        ln = cc * lax.rsqrt(var + EPS) * lng_ref[...] + lnb_ref[...]
        y_conv = (ln * jax.nn.sigmoid(ln)).astype(BF16)
        m = jnp.dot(yp_ref[rows, :], w_ref[0:d_pool, :], preferred_element_type=F32)
        m = m + jnp.dot(y_conv, w_ref[d_pool:, :], preferred_element_type=F32)
        _residual_norms(m, x_ref, gpost_ref, gnext_ref, xo_ref, ho_ref, g_rows, rows)


def _out_proj_l1_kernel(y_ref, x_ref, w_ref, gpost_ref, gnext_ref, xo_ref, ho_ref,
                        *, bm, sub, g_rows):
    for s0 in range(0, bm, sub):
        rows = slice(s0, s0 + sub)
        m = jnp.dot(y_ref[rows, :], w_ref[...], preferred_element_type=F32)
        _residual_norms(m, x_ref, gpost_ref, gnext_ref, xo_ref, ho_ref, g_rows, rows)


def _out_proj(ys, x, w_bf, g_post, g_next, ln=None, *, bm, sub=256):
    assert bm % sub == 0
    (gp_all, gp_row), (gn_all, gn_row) = g_post, g_next
    g_rows = (gp_row, gn_row)
    m, d = x.shape
    k = w_bf.shape[0]
    row = lambda i: (i, 0)
    fixed = lambda i: (0, 0)
    vec = lambda a: a.reshape(1, -1)
    gain_specs = [_const_spec(gp_all.shape, fixed), _const_spec(gn_all.shape, fixed)]
    if ln is None:
        (y,) = ys
        kern = functools.partial(_out_proj_l1_kernel, bm=bm, sub=sub, g_rows=g_rows)
        ins = [y, x, w_bf, gp_all, gn_all]
        in_specs = [pl.BlockSpec((bm, k), row), pl.BlockSpec((bm, d), row),
                    _const_spec((k, d), fixed)] + gain_specs
        tile_bytes = bm * k * 2
    else:
        y_pool, c = ys
        d_pool, d_conv = y_pool.shape[1], c.shape[1]
        kern = functools.partial(_out_proj_l0_kernel, d_pool=d_pool, bm=bm, sub=sub, g_rows=g_rows)
        ins = [y_pool, c, x, w_bf, vec(ln[0]), vec(ln[1]), gp_all, gn_all]
        in_specs = [pl.BlockSpec((bm, d_pool), row), pl.BlockSpec((bm, d_conv), row),
                    pl.BlockSpec((bm, d), row), _const_spec((k, d), fixed),
                    _const_spec((1, d_conv), fixed), _const_spec((1, d_conv), fixed)] + gain_specs
        tile_bytes = bm * d_pool * 2 + bm * d_conv * 4
    vmem = k * d * 2 + 2 * tile_bytes + 2 * bm * d * (4 + 4 + 2) + 5 * bm * d * 4
    return pl.pallas_call(
        kern,
        grid=(m // bm,),
        in_specs=in_specs,
        out_specs=[pl.BlockSpec((bm, d), row), pl.BlockSpec((bm, d), row)],
        out_shape=[jax.ShapeDtypeStruct((m, d), F32), jax.ShapeDtypeStruct((m, d), BF16)],
        compiler_params=_params(("arbitrary",), vmem),
        name="out_proj_l0" if ln is not None else "out_proj_l1",
    )(*ins)


def _ffn_up_kernel(h_ref, w_ref, w2_ref, o_ref, w2bf_ref, wbf, *, bm, sub):
    @pl.when(pl.program_id(1) == 0)
    def _():
        _cast_weight(wbf, w_ref)

    w2bf_ref[...] = w2_ref[...].astype(BF16)
    for s0 in range(0, bm, sub):
        rows = slice(s0, s0 + sub)
        a = jnp.maximum(jnp.dot(h_ref[rows, :], wbf[...], preferred_element_type=F32), 0.0)
        o_ref[rows, :] = (a * a).astype(o_ref.dtype)


def _ffn_up(h, w1, w2, layer, *, bm, bn):
    m, d = h.shape
    f = w1.shape[2]
    nj, ni = f // bn, m // bm
    slab = f // (nj * ni)
    slab_index = lambda j, i: (j * ni + i, 0)
    sub = min(bm, 512)
    vmem = (2 * d * bn * 4 + d * bn * 2 + 2 * bm * d * 2 + 2 * bm * bn * 2 + 3 * sub * bn * 4
            + 2 * slab * d * 6)
    return pl.pallas_call(
        functools.partial(_ffn_up_kernel, bm=bm, sub=sub),
        grid=(nj, ni),
        in_specs=[pl.BlockSpec((bm, d), lambda j, i: (i, 0)),
                  pl.BlockSpec((None, d, bn), lambda j, i: (layer, 0, j)),
                  pl.BlockSpec((None, slab, d), lambda j, i: (layer, j * ni + i, 0))],
        out_specs=[pl.BlockSpec((bm, bn), lambda j, i: (i, j)),
                   pl.BlockSpec((slab, d), slab_index)],
        out_shape=[jax.ShapeDtypeStruct((m, f), BF16),
                   jax.ShapeDtypeStruct((f, d), BF16)],
        scratch_shapes=[pltpu.VMEM((d, bn), BF16)],
        compiler_params=_params(("arbitrary", "arbitrary"), vmem),
        name="ffn_up",
    )(h, w1, w2)


def _ffn_down_kernel(a_ref, w_hbm, x_ref, gpost_ref, gnext_ref, xo_ref, *rest, with_next, n_tiles,
                     n_chunks, g_rows):
    ho_ref = rest[0] if with_next else None
    m_buf, w_ref, sems = rest[-3:]
    kc = w_ref.shape[0] // n_chunks

    def chunk_copy(c):
        rows = pl.ds(c * kc, kc)
        return pltpu.make_async_copy(w_hbm.at[rows, :], w_ref.at[rows, :], sems.at[c])

    def first_matmul():
        for c in range(n_chunks):
            chunk_copy(c).start()
        for c in range(n_chunks):
            chunk_copy(c).wait()
            part = jnp.dot(a_ref[:, c * kc:(c + 1) * kc], w_ref[c * kc:(c + 1) * kc, :],
                           preferred_element_type=F32)
            if c == 0:
                m_buf[...] = part
            else:
                m_buf[...] += part

    def matmul():
        m_buf[...] = jnp.dot(a_ref[...], w_ref[...], preferred_element_type=F32)

    def norms():
        _residual_norms(m_buf[...], x_ref, gpost_ref, gnext_ref, xo_ref, ho_ref, g_rows)

    _lagged_steps(pl.program_id(0), n_tiles, matmul, norms, produce_first=first_matmul)


def _ffn_down(a, w2_bf, x, g_post, g_next, *, bm):
    m, f = a.shape
    d = w2_bf.shape[1]
    n_tiles = m // bm
    with_next = g_next is not None
    if not with_next:
        g_next = g_post
    (gp_all, gp_row), (gn_all, gn_row) = g_post, g_next
    ahead = lambda i: (jnp.minimum(i, n_tiles - 1), 0)
    behind = lambda i: (jnp.maximum(i - 1, 0), 0)
    fixed = lambda i: (0, 0)
    out_specs = [pl.BlockSpec((bm, d), behind)]
    out_shape = [jax.ShapeDtypeStruct((m, d), F32)]
    if with_next:
        out_specs.append(pl.BlockSpec((bm, d), behind))
        out_shape.append(jax.ShapeDtypeStruct((m, d), BF16))
    vmem = f * d * 2 + 2 * bm * f * 2 + 2 * bm * d * (4 + 4 + 2) + 4 * bm * d * 4
    n_chunks = 8
    assert f % n_chunks == 0
    outs = pl.pallas_call(
        functools.partial(_ffn_down_kernel, with_next=with_next, n_tiles=n_tiles,
                          n_chunks=n_chunks, g_rows=(gp_row, gn_row)),
        grid=(n_tiles + 1,),
        in_specs=[pl.BlockSpec((bm, f), ahead),
                  pl.BlockSpec(memory_space=pl.ANY),
                  pl.BlockSpec((bm, d), behind),
                  _const_spec(gp_all.shape, fixed), _const_spec(gn_all.shape, fixed)],
        out_specs=out_specs,
        out_shape=out_shape,
        scratch_shapes=[pltpu.VMEM((bm, d), F32),
                        pltpu.VMEM((f, d), BF16),
                        pltpu.SemaphoreType.DMA((n_chunks,))],
        compiler_params=_params(("arbitrary",), vmem),
        name="ffn_down",
    )(a, w2_bf, x, gp_all, gn_all)
    return outs if with_next else (outs[0], None)


def _gated_conv_kernel(h_ref, wb_ref, wc_ref, wu_ref, cw_ref, wo_ref, o_ref, wobf_ref,
                       wall, pbuf, *, bm, bn, seq, halo, taps, sub):
    i = pl.program_id(1)

    @pl.when(i == 0)
    def _():
        _cast_weight(wall, wc_ref, col0=0)
        _cast_weight(wall, wu_ref, col0=bn)
        _cast_weight(wall, wb_ref, col0=2 * bn)

    wobf_ref[...] = wo_ref[...].astype(BF16)
    _load_history(pbuf, i, bm=bm, seq=seq, halo=halo)
    cw = cw_ref[...]
    for s0 in range(0, bm, sub):
        z = jnp.dot(h_ref[s0:s0 + sub, :], wall[...], preferred_element_type=F32)
        pbuf[halo + s0:halo + s0 + sub, :] = z[:, 0:bn] * z[:, bn:2 * bn]
        conv = cw[taps - 1:taps, :] * pbuf[halo + s0:halo + s0 + sub, :]
        for k in range(taps - 1):
            start = halo + s0 - (taps - 1) + k
            conv = conv + cw[k:k + 1, :] * pbuf[start:start + sub, :]
        o_ref[s0:s0 + sub, :] = (z[:, 2 * bn:] * conv).astype(o_ref.dtype)


def _gated_conv_branch(h, w_in, conv_w, w_out, *, bm, bn, seq):
    conv_w, conv_idx = conv_w
    m, d = h.shape
    _, taps, d_short = conv_w.shape
    halo = _halo_rows(taps - 1)
    nj, ni = d_short // bn, m // bm
    ko, do = w_out.shape
    slab = ko // (nj * ni)
    wo_in, wo_out = _cast_slab_specs(slab, do, ni)
    assert seq % bm == 0 and m % seq == 0
    assert ko % (nj * ni) == 0 and d_short % bn == 0
    kern = functools.partial(_gated_conv_kernel, bm=bm, bn=bn, seq=seq, halo=halo, taps=taps,
                             sub=min(bm, 512))
    vmem = (3 * 2 * d * bn * 4 + 3 * d * bn * 2 + 2 * bm * d * 2 + 2 * bm * bn * 2
            + (halo + bm) * bn * 4 + 5 * bm * bn * 4 + 2 * slab * do * 6)
    return pl.pallas_call(
        kern,
        grid=(nj, ni),
        in_specs=[pl.BlockSpec((bm, d), lambda j, i: (i, 0)),
                  pl.BlockSpec((d, bn), lambda j, i: (0, j)),
                  pl.BlockSpec((d, bn), lambda j, i: (0, nj + j)),
                  pl.BlockSpec((d, bn), lambda j, i: (0, 2 * nj + j)),
                  pl.BlockSpec((None, taps, bn), lambda j, i: (conv_idx, 0, j)),
                  wo_in],
        out_specs=[pl.BlockSpec((bm, bn), lambda j, i: (i, j)), wo_out],
        out_shape=[jax.ShapeDtypeStruct((m, d_short), BF16),
                   jax.ShapeDtypeStruct((ko, do), BF16)],
        scratch_shapes=[pltpu.VMEM((d, 3 * bn), BF16), pltpu.VMEM((halo + bm, bn), F32)],
        compiler_params=_params(("arbitrary", "arbitrary"), vmem),
        name="gated_conv_branch",
    )(h, w_in, w_in, w_in, conv_w, w_out)


def kernel(x, mix_pre_g, mix_post_g, ffn_pre_g, ffn_post_g, ab_w_in, pool_w, pool_scale,
           conv_w, conv_b, conv_ln_g, conv_ln_b, ab_w_out, sc_w_in, sc_conv_w, sc_w_out,
           ffn_w1, ffn_w2):
    batch, seq, d = x.shape
    depth = mix_pre_g.shape[0]
    d_pool = pool_scale.shape[1]
    xs = x.reshape(batch * seq, d)
    h = None
    for layer in range(depth):
        i = layer // 2
        if layer % 2 == 0:
            y_pool, h = _pool_branch(xs, (mix_pre_g, layer), ab_w_in[i], pool_w[i], pool_scale[i],
                                     bm=512, seq=seq)
            c, w_out_bf = _glu_conv_branch(h, ab_w_in[i], (conv_w, i), conv_b[i], ab_w_out[i],
                                           d_pool=d_pool, bm=1024, bn=512, seq=seq)
            xs, h = _out_proj((y_pool, c), xs, w_out_bf, (mix_post_g, layer), (ffn_pre_g, layer),
                              ln=(conv_ln_g[i], conv_ln_b[i]), bm=512)
        else:
            y, w_out_bf = _gated_conv_branch(h, sc_w_in[i], (sc_conv_w, i), sc_w_out[i],
                                             bm=1024, bn=512, seq=seq)
            xs, h = _out_proj((y,), xs, w_out_bf, (mix_post_g, layer), (ffn_pre_g, layer), bm=512)
        a, w2_bf = _ffn_up(h, ffn_w1, ffn_w2, layer, bm=2048, bn=1024)
        g_next = (mix_pre_g, layer + 1) if layer + 1 < depth and (layer + 1) % 2 == 1 else None
        xs, h = _ffn_down(a, w2_bf, xs, (ffn_post_g, layer), g_next, bm=256)
    return xs.reshape(batch, seq, d)
```

```python
import functools

import jax
import jax.numpy as jnp
from jax import lax
from jax.experimental import pallas as pl
from jax.experimental.pallas import tpu as pltpu

EPS = 1e-6
POOL_WINDOWS = (2, 4, 8, 16)

V7X_SUBLANES = 8
V7X_LANES = 128
V7X_VMEM_LIMIT_CAP = 60000 * 1024

F32 = jnp.float32
BF16 = jnp.bfloat16


def _halo_rows(n):
    return -(-n // V7X_SUBLANES) * V7X_SUBLANES


def _params(semantics, vmem_bytes):
    assert vmem_bytes <= 1.35 * V7X_VMEM_LIMIT_CAP
    return pltpu.CompilerParams(
        dimension_semantics=semantics,
        vmem_limit_bytes=V7X_VMEM_LIMIT_CAP)


def _const_spec(shape, index_map):
    return pl.BlockSpec(shape, index_map, pipeline_mode=pl.Buffered(1))


def _cast_weight(dst, src, *, col0=0, rows=256):
    n = src.shape[0] // rows
    cols = slice(col0, col0 + src.shape[1])

    def body(c, carry):
        r = pl.multiple_of(c * rows, rows)
        dst[pl.ds(r, rows), cols] = src[pl.ds(r, rows), :].astype(BF16)
        return carry

    lax.fori_loop(0, n, body, 0)


def _rms(x, g):
    ms = jnp.mean(x * x, axis=-1, keepdims=True)
    return x * lax.rsqrt(ms + EPS) * g


def _load_history(buf, i, *, bm, seq, halo):
    first = lax.rem(i * bm, seq) == 0

    @pl.when(first)
    def _():
        buf[..., 0:halo, :] = jnp.zeros(buf.shape[:-2] + (halo, buf.shape[-1]), buf.dtype)

    @pl.when(jnp.logical_not(first))
    def _():
        buf[..., 0:halo, :] = buf[..., bm:bm + halo, :]


def _pool_kernel(x_ref, g_ref, w_ref, pw_ref, ps_ref, o_ref, h_ref, wbf, pwbf, ubuf,
                 *, bm, seq, halo, group, sub):
    i = pl.program_id(0)

    @pl.when(i == 0)
    def _():
        _cast_weight(wbf, w_ref)
        pwbf[...] = pw_ref[...].astype(BF16)

    _load_history(ubuf, i, bm=bm, seq=seq, halo=halo)
    for s0 in range(0, bm, sub):
        rows = slice(s0, s0 + sub)
        h = _rms(x_ref[rows, :], g_ref[...]).astype(BF16)
        h_ref[rows, :] = h
        ubuf[halo + s0:halo + s0 + sub, :] = jnp.dot(h, wbf[...], preferred_element_type=F32)

        pos = lax.rem(i * bm, seq) + s0 + lax.broadcasted_iota(jnp.int32, (sub, 1), 0)
        for g, w in enumerate(POOL_WINDOWS):
            cols = slice(g * group, (g + 1) * group)
            s = ubuf[s0:s0 + halo + sub, cols]
            span = 1
            while span < w:
                s = s + pltpu.roll(s, span, 0)
                span *= 2
            u = ubuf[halo + s0:halo + s0 + sub, cols]
            cnt = jnp.minimum(pos + 1, w).astype(F32)
            pooled = s[halo:, :] / cnt - u
            mixed = jnp.dot(pooled.astype(BF16), pwbf[g], preferred_element_type=F32)
            o_ref[rows, cols] = (mixed * ps_ref[:, cols]).astype(o_ref.dtype)


def _pool_branch(x, g_pre, w_in, pool_w, pool_scale, *, bm, seq):
    m, d = x.shape
    n_groups, group, _ = pool_w.shape
    d_pool = n_groups * group
    halo = _halo_rows(max(POOL_WINDOWS) - 1)
    assert all(w & (w - 1) == 0 for w in POOL_WINDOWS)
    assert seq % bm == 0 and m % seq == 0
    sub = min(bm, 256)
    kern = functools.partial(_pool_kernel, bm=bm, seq=seq, halo=halo, group=group, sub=sub)
    vmem = (d * d_pool * (4 + 2) + 2 * bm * d * (4 + 2) + 2 * bm * d_pool * 2
            + (halo + bm) * d_pool * 4 + 3 * sub * d_pool * 4 + pool_w.size * 6 + 3 * sub * d * 4)
    return pl.pallas_call(
        kern,
        grid=(m // bm,),
        in_specs=[pl.BlockSpec((bm, d), lambda i: (i, 0)),
                  _const_spec((1, d), lambda i: (0, 0)),
                  _const_spec((d, d_pool), lambda i: (0, 0)),
                  _const_spec(pool_w.shape, lambda i: (0, 0, 0)),
                  _const_spec((1, d_pool), lambda i: (0, 0))],
        out_specs=[pl.BlockSpec((bm, d_pool), lambda i: (i, 0)),
                   pl.BlockSpec((bm, d), lambda i: (i, 0))],
        out_shape=[jax.ShapeDtypeStruct((m, d_pool), BF16),
                   jax.ShapeDtypeStruct((m, d), BF16)],
        scratch_shapes=[pltpu.VMEM((d, d_pool), BF16),
                        pltpu.VMEM(pool_w.shape, BF16),
                        pltpu.VMEM((halo + bm, d_pool), F32)],
        compiler_params=_params(("arbitrary",), vmem),
        name="pool_branch",
    )(x, g_pre.reshape(1, d), w_in, pool_w, pool_scale.reshape(1, d_pool))


def _cast_slab_specs(rows, d, ni):
    index = lambda j, i: (j * ni + i, 0)
    return pl.BlockSpec((rows, d), index), pl.BlockSpec((rows, d), index)


def _glu_conv_kernel(h_ref, wv_ref, wg_ref, cw_ref, cb_ref, wo_ref, o_ref, wobf_ref,
                     wall, gbuf, wrap, obuf, *, bm, bn, seq, halo, taps, sub):
    i = pl.program_id(1)

    @pl.when(i == 0)
    def _():
        _cast_weight(wall, wv_ref, col0=0)
        _cast_weight(wall, wg_ref, col0=bn)

    wobf_ref[...] = wo_ref[...].astype(BF16)
    _load_history(gbuf, i, bm=bm, seq=seq, halo=halo)

    n_lane_tiles = bn // V7X_LANES
    for s0 in range(0, bm, sub):
        z = jnp.dot(h_ref[s0:s0 + sub, :], wall[...], preferred_element_type=F32)
        g = z[:, 0:bn] * jax.nn.sigmoid(z[:, bn:])
        for c in range(n_lane_tiles):
            gbuf[c, halo + s0:halo + s0 + sub, :] = g[:, c * V7X_LANES:(c + 1) * V7X_LANES]

    span = (halo + bm) // V7X_SUBLANES
    first_wrapped = span - (taps - 1)
    for c in range(n_lane_tiles):
        cs = slice(c * V7X_LANES, (c + 1) * V7X_LANES)
        for u in range(first_wrapped, span):
            k = u - first_wrapped
            wrap[c, k * V7X_SUBLANES:(k + 1) * V7X_SUBLANES, :] = pltpu.roll(
                gbuf[c, pl.ds(u, V7X_SUBLANES, stride=span), :], 1, 0)
        for v0 in range(span):
            acc = jnp.broadcast_to(cb_ref[:, cs], (V7X_SUBLANES, V7X_LANES))
            for s in range(taps):
                if v0 >= s:
                    src = gbuf[c, pl.ds(v0 - s, V7X_SUBLANES, stride=span), :]
                else:
                    k = v0 - s + taps - 1
                    src = wrap[c, k * V7X_SUBLANES:(k + 1) * V7X_SUBLANES, :]
                acc = acc + cw_ref[taps - 1 - s:taps - s, cs] * src
            obuf[c, pl.ds(v0, V7X_SUBLANES, stride=span), :] = acc
        o_ref[:, cs] = obuf[c, halo:, :]


def _glu_conv_branch(h, w_in, conv_w, conv_b, w_out, *, d_pool, bm, bn, seq):
    m, d = h.shape
    taps, d_conv = conv_w.shape
    halo = _halo_rows(taps - 1)
    nj, ni = d_conv // bn, m // bm
    v_blk0 = d_pool // bn
    g_blk0 = (d_pool + d_conv) // bn
    ko, do = w_out.shape
    slab = ko // (nj * ni)
    wo_in, wo_out = _cast_slab_specs(slab, do, ni)
    sub = 256
    assert bm % sub == 0 and bn % V7X_LANES == 0 and halo >= taps - 1
    assert seq % bm == 0 and m % seq == 0
    assert ko % (nj * ni) == 0 and d_pool % bn == 0 and d_conv % bn == 0
    kern = functools.partial(_glu_conv_kernel, bm=bm, bn=bn, seq=seq, halo=halo, taps=taps, sub=sub)
    n_lane_tiles = bn // V7X_LANES
    win_shape = (n_lane_tiles, halo + bm, V7X_LANES)
    wrap_shape = (n_lane_tiles, (taps - 1) * V7X_SUBLANES, V7X_LANES)
    vmem = (2 * 2 * d * bn * 4 + 2 * d * bn * 2 + 2 * bm * d * 2 + 2 * bm * bn * 4
            + 2 * (halo + bm) * bn * 4 + 4 * sub * bn * 4 + 2 * slab * do * 6
            + 4 * wrap_shape[0] * wrap_shape[1] * wrap_shape[2])
    return pl.pallas_call(
        kern,
        grid=(nj, ni),
        in_specs=[pl.BlockSpec((bm, d), lambda j, i: (i, 0)),
                  pl.BlockSpec((d, bn), lambda j, i: (0, v_blk0 + j)),
                  pl.BlockSpec((d, bn), lambda j, i: (0, g_blk0 + j)),
                  pl.BlockSpec((taps, bn), lambda j, i: (0, j)),
                  pl.BlockSpec((1, bn), lambda j, i: (0, j)),
                  wo_in],
        out_specs=[pl.BlockSpec((bm, bn), lambda j, i: (i, j)), wo_out],
        out_shape=[jax.ShapeDtypeStruct((m, d_conv), F32),
                   jax.ShapeDtypeStruct((ko, do), BF16)],
        scratch_shapes=[pltpu.VMEM((d, 2 * bn), BF16),
                        pltpu.VMEM(win_shape, F32),
                        pltpu.VMEM(wrap_shape, F32),
                        pltpu.VMEM(win_shape, F32)],
        compiler_params=_params(("arbitrary", "arbitrary"), vmem),
        name="glu_conv_branch",
    )(h, w_in, w_in, conv_w, conv_b.reshape(1, d_conv), w_out)


def _residual_norms(m, x_ref, gpost_ref, gnext_ref, xo_ref, ho_ref, rows=slice(None)):
    x_new = x_ref[rows, :] + _rms(m, gpost_ref[...])
    xo_ref[rows, :] = x_new
    if ho_ref is not None:
        ho_ref[rows, :] = _rms(x_new, gnext_ref[...]).astype(ho_ref.dtype)


def _lagged_steps(i, n_tiles, produce, consume, produce_first=None):
    @pl.when(i == 0)
    def _():
        (produce_first or produce)()

    @pl.when(jnp.logical_and(i > 0, i < n_tiles))
    def _():
        consume()
        produce()

    @pl.when(i == n_tiles)
    def _():
        consume()


def _out_proj_l0_kernel(yp_ref, c_ref, x_ref, w_ref, lng_ref, lnb_ref, gpost_ref, gnext_ref,
                        xo_ref, ho_ref, *, d_pool, bm, sub):
    for s0 in range(0, bm, sub):
        rows = slice(s0, s0 + sub)
        c = c_ref[rows, :]
        mu = jnp.mean(c, axis=-1, keepdims=True)
        cc = c - mu
        var = jnp.mean(cc * cc, axis=-1, keepdims=True)
        ln = cc * lax.rsqrt(var + EPS) * lng_ref[...] + lnb_ref[...]
        y_conv = (ln * jax.nn.sigmoid(ln)).astype(BF16)
        m = jnp.dot(yp_ref[rows, :], w_ref[0:d_pool, :], preferred_element_type=F32)
        m = m + jnp.dot(y_conv, w_ref[d_pool:, :], preferred_element_type=F32)
        _residual_norms(m, x_ref, gpost_ref, gnext_ref, xo_ref, ho_ref, rows)


def _out_proj_l1_kernel(y_ref, x_ref, w_ref, gpost_ref, gnext_ref, xo_ref, ho_ref, *, bm, sub):
    for s0 in range(0, bm, sub):
        rows = slice(s0, s0 + sub)
        m = jnp.dot(y_ref[rows, :], w_ref[...], preferred_element_type=F32)
        _residual_norms(m, x_ref, gpost_ref, gnext_ref, xo_ref, ho_ref, rows)


def _out_proj(ys, x, w_bf, g_post, g_next, ln=None, *, bm, sub=256):
    assert bm % sub == 0
    m, d = x.shape
    k = w_bf.shape[0]
    row = lambda i: (i, 0)
    fixed = lambda i: (0, 0)
    vec = lambda a: a.reshape(1, -1)
    if ln is None:
        (y,) = ys
        kern = functools.partial(_out_proj_l1_kernel, bm=bm, sub=sub)
        ins = [y, x, w_bf, vec(g_post), vec(g_next)]
        in_specs = [pl.BlockSpec((bm, k), row), pl.BlockSpec((bm, d), row),
                    _const_spec((k, d), fixed), _const_spec((1, d), fixed), _const_spec((1, d), fixed)]
        tile_bytes = bm * k * 2
    else:
        y_pool, c = ys
        d_pool, d_conv = y_pool.shape[1], c.shape[1]
        kern = functools.partial(_out_proj_l0_kernel, d_pool=d_pool, bm=bm, sub=sub)
        ins = [y_pool, c, x, w_bf, vec(ln[0]), vec(ln[1]), vec(g_post), vec(g_next)]
        in_specs = [pl.BlockSpec((bm, d_pool), row), pl.BlockSpec((bm, d_conv), row),
                    pl.BlockSpec((bm, d), row), _const_spec((k, d), fixed),
                    _const_spec((1, d_conv), fixed), _const_spec((1, d_conv), fixed),
                    _const_spec((1, d), fixed), _const_spec((1, d), fixed)]
        tile_bytes = bm * d_pool * 2 + bm * d_conv * 4
    vmem = k * d * 2 + 2 * tile_bytes + 2 * bm * d * (4 + 4 + 2) + 5 * bm * d * 4
    return pl.pallas_call(
        kern,
        grid=(m // bm,),
        in_specs=in_specs,
        out_specs=[pl.BlockSpec((bm, d), row), pl.BlockSpec((bm, d), row)],
        out_shape=[jax.ShapeDtypeStruct((m, d), F32), jax.ShapeDtypeStruct((m, d), BF16)],
        compiler_params=_params(("arbitrary",), vmem),
        name="out_proj_l0" if ln is not None else "out_proj_l1",
    )(*ins)


def _ffn_up_kernel(h_ref, w_ref, w2_ref, o_ref, w2bf_ref, wbf, *, bm, sub):
    @pl.when(pl.program_id(1) == 0)
    def _():
        _cast_weight(wbf, w_ref)

    w2bf_ref[...] = w2_ref[...].astype(BF16)
    for s0 in range(0, bm, sub):
        rows = slice(s0, s0 + sub)
        a = jnp.maximum(jnp.dot(h_ref[rows, :], wbf[...], preferred_element_type=F32), 0.0)
        o_ref[rows, :] = (a * a).astype(o_ref.dtype)


def _ffn_up(h, w1, w2, layer, *, bm, bn):
    m, d = h.shape
    f = w1.shape[2]
    nj, ni = f // bn, m // bm
    slab = f // (nj * ni)
    slab_index = lambda j, i: (j * ni + i, 0)
    sub = min(bm, 512)
    vmem = (2 * d * bn * 4 + d * bn * 2 + 2 * bm * d * 2 + 2 * bm * bn * 2 + 3 * sub * bn * 4
            + 2 * slab * d * 6)
    return pl.pallas_call(
        functools.partial(_ffn_up_kernel, bm=bm, sub=sub),
        grid=(nj, ni),
        in_specs=[pl.BlockSpec((bm, d), lambda j, i: (i, 0)),
                  pl.BlockSpec((None, d, bn), lambda j, i: (layer, 0, j)),
                  pl.BlockSpec((None, slab, d), lambda j, i: (layer, j * ni + i, 0))],
        out_specs=[pl.BlockSpec((bm, bn), lambda j, i: (i, j)),
                   pl.BlockSpec((slab, d), slab_index)],
        out_shape=[jax.ShapeDtypeStruct((m, f), BF16),
                   jax.ShapeDtypeStruct((f, d), BF16)],
        scratch_shapes=[pltpu.VMEM((d, bn), BF16)],
        compiler_params=_params(("arbitrary", "arbitrary"), vmem),
        name="ffn_up",
    )(h, w1, w2)


def _ffn_down_kernel(a_ref, w_hbm, x_ref, gpost_ref, gnext_ref, xo_ref, *rest, with_next, n_tiles,
                     n_chunks):
    ho_ref = rest[0] if with_next else None
    m_buf, w_ref, sems = rest[-3:]
    kc = w_ref.shape[0] // n_chunks

    def chunk_copy(c):
        rows = pl.ds(c * kc, kc)
        return pltpu.make_async_copy(w_hbm.at[rows, :], w_ref.at[rows, :], sems.at[c])

    def first_matmul():
        for c in range(n_chunks):
            chunk_copy(c).start()
        for c in range(n_chunks):
            chunk_copy(c).wait()
            part = jnp.dot(a_ref[:, c * kc:(c + 1) * kc], w_ref[c * kc:(c + 1) * kc, :],
                           preferred_element_type=F32)
            if c == 0:
                m_buf[...] = part
            else:
                m_buf[...] += part

    def matmul():
        m_buf[...] = jnp.dot(a_ref[...], w_ref[...], preferred_element_type=F32)

    def norms():
        _residual_norms(m_buf[...], x_ref, gpost_ref, gnext_ref, xo_ref, ho_ref)

    _lagged_steps(pl.program_id(0), n_tiles, matmul, norms, produce_first=first_matmul)


def _ffn_down(a, w2_bf, x, g_post, g_next, *, bm):
    m, f = a.shape
    d = w2_bf.shape[1]
    n_tiles = m // bm
    with_next = g_next is not None
    if not with_next:
        g_next = g_post
    ahead = lambda i: (jnp.minimum(i, n_tiles - 1), 0)
    behind = lambda i: (jnp.maximum(i - 1, 0), 0)
    fixed = lambda i: (0, 0)
    out_specs = [pl.BlockSpec((bm, d), behind)]
    out_shape = [jax.ShapeDtypeStruct((m, d), F32)]
    if with_next:
        out_specs.append(pl.BlockSpec((bm, d), behind))
        out_shape.append(jax.ShapeDtypeStruct((m, d), BF16))
    vmem = f * d * 2 + 2 * bm * f * 2 + 2 * bm * d * (4 + 4 + 2) + 4 * bm * d * 4
    n_chunks = 8
    assert f % n_chunks == 0
    outs = pl.pallas_call(
        functools.partial(_ffn_down_kernel, with_next=with_next, n_tiles=n_tiles,
                          n_chunks=n_chunks),
        grid=(n_tiles + 1,),
        in_specs=[pl.BlockSpec((bm, f), ahead),
                  pl.BlockSpec(memory_space=pl.ANY),
                  pl.BlockSpec((bm, d), behind),
                  _const_spec((1, d), fixed), _const_spec((1, d), fixed)],
        out_specs=out_specs,
        out_shape=out_shape,
        scratch_shapes=[pltpu.VMEM((bm, d), F32),
                        pltpu.VMEM((f, d), BF16),
                        pltpu.SemaphoreType.DMA((n_chunks,))],
        compiler_params=_params(("arbitrary",), vmem),
        name="ffn_down",
    )(a, w2_bf, x, g_post.reshape(1, d), g_next.reshape(1, d))
    return outs if with_next else (outs[0], None)


def _gated_conv_kernel(h_ref, wb_ref, wc_ref, wu_ref, cw_ref, wo_ref, o_ref, wobf_ref,
                       wall, pbuf, *, bm, bn, seq, halo, taps, sub):
    i = pl.program_id(1)

    @pl.when(i == 0)
    def _():
        _cast_weight(wall, wc_ref, col0=0)
        _cast_weight(wall, wu_ref, col0=bn)
        _cast_weight(wall, wb_ref, col0=2 * bn)

    wobf_ref[...] = wo_ref[...].astype(BF16)
    _load_history(pbuf, i, bm=bm, seq=seq, halo=halo)
    cw = cw_ref[...]
    for s0 in range(0, bm, sub):
        z = jnp.dot(h_ref[s0:s0 + sub, :], wall[...], preferred_element_type=F32)
        pbuf[halo + s0:halo + s0 + sub, :] = z[:, 0:bn] * z[:, bn:2 * bn]
        conv = cw[taps - 1:taps, :] * pbuf[halo + s0:halo + s0 + sub, :]
        for k in range(taps - 1):
            start = halo + s0 - (taps - 1) + k
            conv = conv + cw[k:k + 1, :] * pbuf[start:start + sub, :]
        o_ref[s0:s0 + sub, :] = (z[:, 2 * bn:] * conv).astype(o_ref.dtype)


def _gated_conv_branch(h, w_in, conv_w, w_out, *, bm, bn, seq):
    m, d = h.shape
    taps, d_short = conv_w.shape
    halo = _halo_rows(taps - 1)
    nj, ni = d_short // bn, m // bm
    ko, do = w_out.shape
    slab = ko // (nj * ni)
    wo_in, wo_out = _cast_slab_specs(slab, do, ni)
    assert seq % bm == 0 and m % seq == 0
    assert ko % (nj * ni) == 0 and d_short % bn == 0
    kern = functools.partial(_gated_conv_kernel, bm=bm, bn=bn, seq=seq, halo=halo, taps=taps,
                             sub=min(bm, 512))
    vmem = (3 * 2 * d * bn * 4 + 3 * d * bn * 2 + 2 * bm * d * 2 + 2 * bm * bn * 2
            + (halo + bm) * bn * 4 + 5 * bm * bn * 4 + 2 * slab * do * 6)
    return pl.pallas_call(
        kern,
        grid=(nj, ni),
        in_specs=[pl.BlockSpec((bm, d), lambda j, i: (i, 0)),
                  pl.BlockSpec((d, bn), lambda j, i: (0, j)),
                  pl.BlockSpec((d, bn), lambda j, i: (0, nj + j)),
                  pl.BlockSpec((d, bn), lambda j, i: (0, 2 * nj + j)),
                  pl.BlockSpec((taps, bn), lambda j, i: (0, j)),
                  wo_in],
        out_specs=[pl.BlockSpec((bm, bn), lambda j, i: (i, j)), wo_out],
        out_shape=[jax.ShapeDtypeStruct((m, d_short), BF16),
                   jax.ShapeDtypeStruct((ko, do), BF16)],
        scratch_shapes=[pltpu.VMEM((d, 3 * bn), BF16), pltpu.VMEM((halo + bm, bn), F32)],
        compiler_params=_params(("arbitrary", "arbitrary"), vmem),
        name="gated_conv_branch",
    )(h, w_in, w_in, w_in, conv_w, w_out)


def kernel(x, mix_pre_g, mix_post_g, ffn_pre_g, ffn_post_g, ab_w_in, pool_w, pool_scale,
           conv_w, conv_b, conv_ln_g, conv_ln_b, ab_w_out, sc_w_in, sc_conv_w, sc_w_out,
           ffn_w1, ffn_w2):
    batch, seq, d = x.shape
    depth = mix_pre_g.shape[0]
    d_pool = pool_scale.shape[1]
    xs = x.reshape(batch * seq, d)
    h = None
    for layer in range(depth):
        i = layer // 2
        if layer % 2 == 0:
            y_pool, h = _pool_branch(xs, mix_pre_g[layer], ab_w_in[i], pool_w[i], pool_scale[i],
                                     bm=1024, seq=seq)
            c, w_out_bf = _glu_conv_branch(h, ab_w_in[i], conv_w[i], conv_b[i], ab_w_out[i],
                                           d_pool=d_pool, bm=1024, bn=512, seq=seq)
            xs, h = _out_proj((y_pool, c), xs, w_out_bf, mix_post_g[layer], ffn_pre_g[layer],
                              ln=(conv_ln_g[i], conv_ln_b[i]), bm=512)
        else:
            y, w_out_bf = _gated_conv_branch(h, sc_w_in[i], sc_conv_w[i], sc_w_out[i],
                                             bm=1024, bn=512, seq=seq)
            xs, h = _out_proj((y,), xs, w_out_bf, mix_post_g[layer], ffn_pre_g[layer], bm=512,
                              sub=128)
        a, w2_bf = _ffn_up(h, ffn_w1, ffn_w2, layer, bm=2048, bn=1024)
        g_next = mix_pre_g[layer + 1] if layer + 1 < depth and (layer + 1) % 2 == 1 else None
        xs, h = _ffn_down(a, w2_bf, xs, ffn_post_g[layer], g_next, bm=256)
    return xs.reshape(batch, seq, d)
```

```python
import functools

import jax
import jax.numpy as jnp
from jax import lax
from jax.experimental import pallas as pl
from jax.experimental.pallas import tpu as pltpu

EPS = 1e-6
POOL_WINDOWS = (2, 4, 8, 16)

V7X_SUBLANES = 8
V7X_LANES = 128
V7X_VMEM_LIMIT_CAP = 60000 * 1024

F32 = jnp.float32
BF16 = jnp.bfloat16


def _halo_rows(n):
    return -(-n // V7X_SUBLANES) * V7X_SUBLANES


def _params(semantics, vmem_bytes):
    assert vmem_bytes <= 1.35 * V7X_VMEM_LIMIT_CAP
    return pltpu.CompilerParams(
        dimension_semantics=semantics,
        vmem_limit_bytes=V7X_VMEM_LIMIT_CAP)


def _const_spec(shape, index_map):
    return pl.BlockSpec(shape, index_map, pipeline_mode=pl.Buffered(1))


def _cast_weight(dst, src, *, col0=0, rows=256):
    n = src.shape[0] // rows
    cols = slice(col0, col0 + src.shape[1])

    def body(c, carry):
        r = pl.multiple_of(c * rows, rows)
        dst[pl.ds(r, rows), cols] = src[pl.ds(r, rows), :].astype(BF16)
        return carry

    lax.fori_loop(0, n, body, 0)


def _rms(x, g):
    ms = jnp.mean(x * x, axis=-1, keepdims=True)
    return x * lax.rsqrt(ms + EPS) * g


def _load_history(buf, i, *, bm, seq, halo):
    first = lax.rem(i * bm, seq) == 0

    @pl.when(first)
    def _():
        buf[..., 0:halo, :] = jnp.zeros(buf.shape[:-2] + (halo, buf.shape[-1]), buf.dtype)

    @pl.when(jnp.logical_not(first))
    def _():
        buf[..., 0:halo, :] = buf[..., bm:bm + halo, :]


def _pool_kernel(x_ref, g_ref, w_ref, pw_ref, ps_ref, o_ref, h_ref, wbf, pwbf, ubuf,
                 *, bm, seq, halo, group, sub):
    i = pl.program_id(0)

    @pl.when(i == 0)
    def _():
        _cast_weight(wbf, w_ref)
        pwbf[...] = pw_ref[...].astype(BF16)

    _load_history(ubuf, i, bm=bm, seq=seq, halo=halo)
    for s0 in range(0, bm, sub):
        rows = slice(s0, s0 + sub)
        h = _rms(x_ref[rows, :], g_ref[...]).astype(BF16)
        h_ref[rows, :] = h
        ubuf[halo + s0:halo + s0 + sub, :] = jnp.dot(h, wbf[...], preferred_element_type=F32)

        pos = lax.rem(i * bm, seq) + s0 + lax.broadcasted_iota(jnp.int32, (sub, 1), 0)
        for g, w in enumerate(POOL_WINDOWS):
            cols = slice(g * group, (g + 1) * group)
            s = ubuf[s0:s0 + halo + sub, cols]
            span = 1
            while span < w:
                s = s + pltpu.roll(s, span, 0)
                span *= 2
            u = ubuf[halo + s0:halo + s0 + sub, cols]
            cnt = jnp.minimum(pos + 1, w).astype(F32)
            pooled = s[halo:, :] / cnt - u
            mixed = jnp.dot(pooled.astype(BF16), pwbf[g], preferred_element_type=F32)
            o_ref[rows, cols] = (mixed * ps_ref[:, cols]).astype(o_ref.dtype)


def _pool_branch(x, g_pre, w_in, pool_w, pool_scale, *, bm, seq):
    m, d = x.shape
    n_groups, group, _ = pool_w.shape
    d_pool = n_groups * group
    halo = _halo_rows(max(POOL_WINDOWS) - 1)
    assert all(w & (w - 1) == 0 for w in POOL_WINDOWS)
    assert seq % bm == 0 and m % seq == 0
    kern = functools.partial(_pool_kernel, bm=bm, seq=seq, halo=halo, group=group,
                             sub=min(bm, 256))
    vmem = (d * d_pool * (4 + 2) + 2 * bm * d * (4 + 2) + 2 * bm * d_pool * 2
            + (halo + bm) * d_pool * 4 + 3 * bm * d_pool * 4 + pool_w.size * 6 + 3 * bm * d * 4)
    return pl.pallas_call(
        kern,
        grid=(m // bm,),
        in_specs=[pl.BlockSpec((bm, d), lambda i: (i, 0)),
                  _const_spec((1, d), lambda i: (0, 0)),
                  _const_spec((d, d_pool), lambda i: (0, 0)),
                  _const_spec(pool_w.shape, lambda i: (0, 0, 0)),
                  _const_spec((1, d_pool), lambda i: (0, 0))],
        out_specs=[pl.BlockSpec((bm, d_pool), lambda i: (i, 0)),
                   pl.BlockSpec((bm, d), lambda i: (i, 0))],
        out_shape=[jax.ShapeDtypeStruct((m, d_pool), BF16),
                   jax.ShapeDtypeStruct((m, d), BF16)],
        scratch_shapes=[pltpu.VMEM((d, d_pool), BF16),
                        pltpu.VMEM(pool_w.shape, BF16),
                        pltpu.VMEM((halo + bm, d_pool), F32)],
        compiler_params=_params(("arbitrary",), vmem),
        name="pool_branch",
    )(x, g_pre.reshape(1, d), w_in, pool_w, pool_scale.reshape(1, d_pool))


def _cast_slab_specs(rows, d, ni):
    index = lambda j, i: (j * ni + i, 0)
    return pl.BlockSpec((rows, d), index), pl.BlockSpec((rows, d), index)


def _glu_conv_kernel(h_ref, wv_ref, wg_ref, cw_ref, cb_ref, wo_ref, o_ref, wobf_ref,
                     wall, gbuf, wrap, obuf, *, bm, bn, seq, halo, taps, sub):
    i = pl.program_id(1)

    @pl.when(i == 0)
    def _():
        _cast_weight(wall, wv_ref, col0=0)
        _cast_weight(wall, wg_ref, col0=bn)

    wobf_ref[...] = wo_ref[...].astype(BF16)
    _load_history(gbuf, i, bm=bm, seq=seq, halo=halo)

    n_lane_tiles = bn // V7X_LANES
    for s0 in range(0, bm, sub):
        z = jnp.dot(h_ref[s0:s0 + sub, :], wall[...], preferred_element_type=F32)
        g = z[:, 0:bn] * jax.nn.sigmoid(z[:, bn:])
        for c in range(n_lane_tiles):
            gbuf[c, halo + s0:halo + s0 + sub, :] = g[:, c * V7X_LANES:(c + 1) * V7X_LANES]

    span = (halo + bm) // V7X_SUBLANES
    first_wrapped = span - (taps - 1)
    for c in range(n_lane_tiles):
        cs = slice(c * V7X_LANES, (c + 1) * V7X_LANES)
        for u in range(first_wrapped, span):
            k = u - first_wrapped
            wrap[c, k * V7X_SUBLANES:(k + 1) * V7X_SUBLANES, :] = pltpu.roll(
                gbuf[c, pl.ds(u, V7X_SUBLANES, stride=span), :], 1, 0)
        for v0 in range(span):
            acc = jnp.broadcast_to(cb_ref[:, cs], (V7X_SUBLANES, V7X_LANES))
            for s in range(taps):
                if v0 >= s:
                    src = gbuf[c, pl.ds(v0 - s, V7X_SUBLANES, stride=span), :]
                else:
                    k = v0 - s + taps - 1
                    src = wrap[c, k * V7X_SUBLANES:(k + 1) * V7X_SUBLANES, :]
                acc = acc + cw_ref[taps - 1 - s:taps - s, cs] * src
            obuf[c, pl.ds(v0, V7X_SUBLANES, stride=span), :] = acc
        o_ref[:, cs] = obuf[c, halo:, :]


def _glu_conv_branch(h, w_in, conv_w, conv_b, w_out, *, d_pool, bm, bn, seq):
    m, d = h.shape
    taps, d_conv = conv_w.shape
    halo = _halo_rows(taps - 1)
    nj, ni = d_conv // bn, m // bm
    v_blk0 = d_pool // bn
    g_blk0 = (d_pool + d_conv) // bn
    ko, do = w_out.shape
    slab = ko // (nj * ni)
    wo_in, wo_out = _cast_slab_specs(slab, do, ni)
    sub = 256
    assert bm % sub == 0 and bn % V7X_LANES == 0 and halo >= taps - 1
    assert seq % bm == 0 and m % seq == 0
    assert ko % (nj * ni) == 0 and d_pool % bn == 0 and d_conv % bn == 0
    kern = functools.partial(_glu_conv_kernel, bm=bm, bn=bn, seq=seq, halo=halo, taps=taps, sub=sub)
    n_lane_tiles = bn // V7X_LANES
    win_shape = (n_lane_tiles, halo + bm, V7X_LANES)
    wrap_shape = (n_lane_tiles, (taps - 1) * V7X_SUBLANES, V7X_LANES)
    vmem = (2 * 2 * d * bn * 4 + 2 * d * bn * 2 + 2 * bm * d * 2 + 2 * bm * bn * 4
            + 2 * (halo + bm) * bn * 4 + 4 * sub * bn * 4 + 2 * slab * do * 6
            + 4 * wrap_shape[0] * wrap_shape[1] * wrap_shape[2])
    return pl.pallas_call(
        kern,
        grid=(nj, ni),
        in_specs=[pl.BlockSpec((bm, d), lambda j, i: (i, 0)),
                  pl.BlockSpec((d, bn), lambda j, i: (0, v_blk0 + j)),
                  pl.BlockSpec((d, bn), lambda j, i: (0, g_blk0 + j)),
                  pl.BlockSpec((taps, bn), lambda j, i: (0, j)),
                  pl.BlockSpec((1, bn), lambda j, i: (0, j)),
                  wo_in],
        out_specs=[pl.BlockSpec((bm, bn), lambda j, i: (i, j)), wo_out],
        out_shape=[jax.ShapeDtypeStruct((m, d_conv), F32),
                   jax.ShapeDtypeStruct((ko, do), BF16)],
        scratch_shapes=[pltpu.VMEM((d, 2 * bn), BF16),
                        pltpu.VMEM(win_shape, F32),
                        pltpu.VMEM(wrap_shape, F32),
                        pltpu.VMEM(win_shape, F32)],
        compiler_params=_params(("arbitrary", "arbitrary"), vmem),
        name="glu_conv_branch",
    )(h, w_in, w_in, conv_w, conv_b.reshape(1, d_conv), w_out)


def _residual_norms(m, x_ref, gpost_ref, gnext_ref, xo_ref, ho_ref, rows=slice(None)):
    x_new = x_ref[rows, :] + _rms(m, gpost_ref[...])
    xo_ref[rows, :] = x_new
    if ho_ref is not None:
        ho_ref[rows, :] = _rms(x_new, gnext_ref[...]).astype(ho_ref.dtype)


def _lagged_steps(i, n_tiles, produce, consume, produce_first=None):
    @pl.when(i == 0)
    def _():
        (produce_first or produce)()

    @pl.when(jnp.logical_and(i > 0, i < n_tiles))
    def _():
        consume()
        produce()

    @pl.when(i == n_tiles)
    def _():
        consume()


def _ring_tile(x_hbm, ring, sems, *, bm, n_tiles):
    depth = ring.shape[0]
    i = pl.program_id(0)

    def copy(step, slot):
        rows = pl.ds(pl.multiple_of(step * bm, bm), bm)
        return pltpu.make_async_copy(x_hbm.at[rows, :], ring.at[slot], sems.at[slot])

    @pl.when(i == 0)
    def _():
        for s in range(min(depth - 1, n_tiles)):
            copy(s, s).start()

    ahead = i + depth - 1

    @pl.when(ahead < n_tiles)
    def _():
        copy(ahead, lax.rem(ahead, depth)).start()

    slot = lax.rem(i, depth)
    copy(i, slot).wait()
    return ring.at[slot]


def _out_proj_l0_kernel(yp_ref, c_ref, x_hbm, w_ref, lng_ref, lnb_ref, gpost_ref, gnext_ref,
                        xo_ref, ho_ref, ring, sems, *, d_pool, bm, sub, n_tiles):
    x_ref = _ring_tile(x_hbm, ring, sems, bm=bm, n_tiles=n_tiles)
    for s0 in range(0, bm, sub):
        rows = slice(s0, s0 + sub)
        c = c_ref[rows, :]
        mu = jnp.mean(c, axis=-1, keepdims=True)
        cc = c - mu
        var = jnp.mean(cc * cc, axis=-1, keepdims=True)
        ln = cc * lax.rsqrt(var + EPS) * lng_ref[...] + lnb_ref[...]
        y_conv = (ln * jax.nn.sigmoid(ln)).astype(BF16)
        m = jnp.dot(yp_ref[rows, :], w_ref[0:d_pool, :], preferred_element_type=F32)
        m = m + jnp.dot(y_conv, w_ref[d_pool:, :], preferred_element_type=F32)
        _residual_norms(m, x_ref, gpost_ref, gnext_ref, xo_ref, ho_ref, rows)


def _out_proj_l1_kernel(y_ref, x_hbm, w_ref, gpost_ref, gnext_ref, xo_ref, ho_ref, ring, sems,
                        *, bm, sub, n_tiles):
    x_ref = _ring_tile(x_hbm, ring, sems, bm=bm, n_tiles=n_tiles)
    for s0 in range(0, bm, sub):
        rows = slice(s0, s0 + sub)
        m = jnp.dot(y_ref[rows, :], w_ref[...], preferred_element_type=F32)
        _residual_norms(m, x_ref, gpost_ref, gnext_ref, xo_ref, ho_ref, rows)


def _out_proj(ys, x, w_bf, g_post, g_next, ln=None, *, bm, sub=256):
    assert bm % sub == 0
    m, d = x.shape
    k = w_bf.shape[0]
    row = lambda i: (i, 0)
    fixed = lambda i: (0, 0)
    vec = lambda a: a.reshape(1, -1)
    if ln is None:
        (y,) = ys
        kern = functools.partial(_out_proj_l1_kernel, bm=bm, sub=sub, n_tiles=m // bm)
        ins = [y, x, w_bf, vec(g_post), vec(g_next)]
        in_specs = [pl.BlockSpec((bm, k), row), pl.BlockSpec(memory_space=pl.ANY),
                    _const_spec((k, d), fixed), _const_spec((1, d), fixed), _const_spec((1, d), fixed)]
        tile_bytes = bm * k * 2
    else:
        y_pool, c = ys
        d_pool, d_conv = y_pool.shape[1], c.shape[1]
        kern = functools.partial(_out_proj_l0_kernel, d_pool=d_pool, bm=bm, sub=sub,
                                 n_tiles=m // bm)
        ins = [y_pool, c, x, w_bf, vec(ln[0]), vec(ln[1]), vec(g_post), vec(g_next)]
        in_specs = [pl.BlockSpec((bm, d_pool), row), pl.BlockSpec((bm, d_conv), row),
                    pl.BlockSpec(memory_space=pl.ANY), _const_spec((k, d), fixed),
                    _const_spec((1, d_conv), fixed), _const_spec((1, d_conv), fixed),
                    _const_spec((1, d), fixed), _const_spec((1, d), fixed)]
        tile_bytes = bm * d_pool * 2 + bm * d_conv * 4
    vmem = k * d * 2 + 2 * tile_bytes + 2 * bm * d * (4 + 4 + 2) + 5 * bm * d * 4
    return pl.pallas_call(
        kern,
        grid=(m // bm,),
        in_specs=in_specs,
        out_specs=[pl.BlockSpec((bm, d), row), pl.BlockSpec((bm, d), row)],
        out_shape=[jax.ShapeDtypeStruct((m, d), F32), jax.ShapeDtypeStruct((m, d), BF16)],
        scratch_shapes=[pltpu.VMEM((3, bm, d), F32), pltpu.SemaphoreType.DMA((3,))],
        compiler_params=_params(("arbitrary",), vmem),
        name="out_proj_l0" if ln is not None else "out_proj_l1",
    )(*ins)


def _ffn_up_kernel(h_ref, w_ref, w2_ref, o_ref, w2bf_ref, wbf, *, bm, sub):
    @pl.when(pl.program_id(1) == 0)
    def _():
        _cast_weight(wbf, w_ref)

    w2bf_ref[...] = w2_ref[...].astype(BF16)
    for s0 in range(0, bm, sub):
        rows = slice(s0, s0 + sub)
        a = jnp.maximum(jnp.dot(h_ref[rows, :], wbf[...], preferred_element_type=F32), 0.0)
        o_ref[rows, :] = (a * a).astype(o_ref.dtype)


def _ffn_up(h, w1, w2, layer, *, bm, bn):
    m, d = h.shape
    f = w1.shape[2]
    nj, ni = f // bn, m // bm
    slab = f // (nj * ni)
    slab_index = lambda j, i: (j * ni + i, 0)
    sub = min(bm, 512)
    vmem = (2 * d * bn * 4 + d * bn * 2 + 2 * bm * d * 2 + 2 * bm * bn * 2 + 3 * sub * bn * 4
            + 2 * slab * d * 6)
    return pl.pallas_call(
        functools.partial(_ffn_up_kernel, bm=bm, sub=sub),
        grid=(nj, ni),
        in_specs=[pl.BlockSpec((bm, d), lambda j, i: (i, 0)),
                  pl.BlockSpec((None, d, bn), lambda j, i: (layer, 0, j)),
                  pl.BlockSpec((None, slab, d), lambda j, i: (layer, j * ni + i, 0))],
        out_specs=[pl.BlockSpec((bm, bn), lambda j, i: (i, j)),
                   pl.BlockSpec((slab, d), slab_index)],
        out_shape=[jax.ShapeDtypeStruct((m, f), BF16),
                   jax.ShapeDtypeStruct((f, d), BF16)],
        scratch_shapes=[pltpu.VMEM((d, bn), BF16)],
        compiler_params=_params(("arbitrary", "arbitrary"), vmem),
        name="ffn_up",
    )(h, w1, w2)


def _ffn_down_kernel(a_ref, w_hbm, x_ref, gpost_ref, gnext_ref, xo_ref, *rest, with_next, n_tiles,
                     n_chunks):
    ho_ref = rest[0] if with_next else None
    m_buf, w_ref, sems = rest[-3:]
    kc = w_ref.shape[0] // n_chunks

    def chunk_copy(c):
        rows = pl.ds(c * kc, kc)
        return pltpu.make_async_copy(w_hbm.at[rows, :], w_ref.at[rows, :], sems.at[c])

    def first_matmul():
        for c in range(n_chunks):
            chunk_copy(c).start()
        for c in range(n_chunks):
            chunk_copy(c).wait()
            part = jnp.dot(a_ref[:, c * kc:(c + 1) * kc], w_ref[c * kc:(c + 1) * kc, :],
                           preferred_element_type=F32)
            if c == 0:
                m_buf[...] = part
            else:
                m_buf[...] += part

    def matmul():
        m_buf[...] = jnp.dot(a_ref[...], w_ref[...], preferred_element_type=F32)

    def norms():
        _residual_norms(m_buf[...], x_ref, gpost_ref, gnext_ref, xo_ref, ho_ref)

    _lagged_steps(pl.program_id(0), n_tiles, matmul, norms, produce_first=first_matmul)


def _ffn_down(a, w2_bf, x, g_post, g_next, *, bm):
    m, f = a.shape
    d = w2_bf.shape[1]
    n_tiles = m // bm
    with_next = g_next is not None
    if not with_next:
        g_next = g_post
    ahead = lambda i: (jnp.minimum(i, n_tiles - 1), 0)
    behind = lambda i: (jnp.maximum(i - 1, 0), 0)
    fixed = lambda i: (0, 0)
    out_specs = [pl.BlockSpec((bm, d), behind)]
    out_shape = [jax.ShapeDtypeStruct((m, d), F32)]
    if with_next:
        out_specs.append(pl.BlockSpec((bm, d), behind))
        out_shape.append(jax.ShapeDtypeStruct((m, d), BF16))
    vmem = f * d * 2 + 2 * bm * f * 2 + 2 * bm * d * (4 + 4 + 2) + 4 * bm * d * 4
    n_chunks = 8
    assert f % n_chunks == 0
    outs = pl.pallas_call(
        functools.partial(_ffn_down_kernel, with_next=with_next, n_tiles=n_tiles,
                          n_chunks=n_chunks),
        grid=(n_tiles + 1,),
        in_specs=[pl.BlockSpec((bm, f), ahead),
                  pl.BlockSpec(memory_space=pl.ANY),
                  pl.BlockSpec((bm, d), behind),
                  _const_spec((1, d), fixed), _const_spec((1, d), fixed)],
        out_specs=out_specs,
        out_shape=out_shape,
        scratch_shapes=[pltpu.VMEM((bm, d), F32),
                        pltpu.VMEM((f, d), BF16),
                        pltpu.SemaphoreType.DMA((n_chunks,))],
        compiler_params=_params(("arbitrary",), vmem),
        name="ffn_down",
    )(a, w2_bf, x, g_post.reshape(1, d), g_next.reshape(1, d))
    return outs if with_next else (outs[0], None)


def _gated_conv_kernel(h_ref, wb_ref, wc_ref, wu_ref, cw_ref, wo_ref, o_ref, wobf_ref,
                       wall, pbuf, *, bm, bn, seq, halo, taps, sub):
    i = pl.program_id(1)

    @pl.when(i == 0)
    def _():
        _cast_weight(wall, wc_ref, col0=0)
        _cast_weight(wall, wu_ref, col0=bn)
        _cast_weight(wall, wb_ref, col0=2 * bn)

    wobf_ref[...] = wo_ref[...].astype(BF16)
    _load_history(pbuf, i, bm=bm, seq=seq, halo=halo)
    cw = cw_ref[...]
    for s0 in range(0, bm, sub):
        z = jnp.dot(h_ref[s0:s0 + sub, :], wall[...], preferred_element_type=F32)
        pbuf[halo + s0:halo + s0 + sub, :] = z[:, 0:bn] * z[:, bn:2 * bn]
        conv = cw[taps - 1:taps, :] * pbuf[halo + s0:halo + s0 + sub, :]
        for k in range(taps - 1):
            start = halo + s0 - (taps - 1) + k
            conv = conv + cw[k:k + 1, :] * pbuf[start:start + sub, :]
        o_ref[s0:s0 + sub, :] = (z[:, 2 * bn:] * conv).astype(o_ref.dtype)


def _gated_conv_branch(h, w_in, conv_w, w_out, *, bm, bn, seq):
    m, d = h.shape
    taps, d_short = conv_w.shape
    halo = _halo_rows(taps - 1)
    nj, ni = d_short // bn, m // bm
    ko, do = w_out.shape
    slab = ko // (nj * ni)
    wo_in, wo_out = _cast_slab_specs(slab, do, ni)
    assert seq % bm == 0 and m % seq == 0
    assert ko % (nj * ni) == 0 and d_short % bn == 0
    kern = functools.partial(_gated_conv_kernel, bm=bm, bn=bn, seq=seq, halo=halo, taps=taps,
                             sub=min(bm, 512))
    vmem = (3 * 2 * d * bn * 4 + 3 * d * bn * 2 + 2 * bm * d * 2 + 2 * bm * bn * 2
            + (halo + bm) * bn * 4 + 5 * bm * bn * 4 + 2 * slab * do * 6)
    return pl.pallas_call(
        kern,
        grid=(nj, ni),
        in_specs=[pl.BlockSpec((bm, d), lambda j, i: (i, 0)),
                  pl.BlockSpec((d, bn), lambda j, i: (0, j)),
                  pl.BlockSpec((d, bn), lambda j, i: (0, nj + j)),
                  pl.BlockSpec((d, bn), lambda j, i: (0, 2 * nj + j)),
                  pl.BlockSpec((taps, bn), lambda j, i: (0, j)),
                  wo_in],
        out_specs=[pl.BlockSpec((bm, bn), lambda j, i: (i, j)), wo_out],
        out_shape=[jax.ShapeDtypeStruct((m, d_short), BF16),
                   jax.ShapeDtypeStruct((ko, do), BF16)],
        scratch_shapes=[pltpu.VMEM((d, 3 * bn), BF16), pltpu.VMEM((halo + bm, bn), F32)],
        compiler_params=_params(("arbitrary", "arbitrary"), vmem),
        name="gated_conv_branch",
    )(h, w_in, w_in, w_in, conv_w, w_out)


def kernel(x, mix_pre_g, mix_post_g, ffn_pre_g, ffn_post_g, ab_w_in, pool_w, pool_scale,
           conv_w, conv_b, conv_ln_g, conv_ln_b, ab_w_out, sc_w_in, sc_conv_w, sc_w_out,
           ffn_w1, ffn_w2):
    batch, seq, d = x.shape
    depth = mix_pre_g.shape[0]
    d_pool = pool_scale.shape[1]
    xs = x.reshape(batch * seq, d)
    h = None
    for layer in range(depth):
        i = layer // 2
        if layer % 2 == 0:
            y_pool, h = _pool_branch(xs, mix_pre_g[layer], ab_w_in[i], pool_w[i], pool_scale[i],
                                     bm=512, seq=seq)
            c, w_out_bf = _glu_conv_branch(h, ab_w_in[i], conv_w[i], conv_b[i], ab_w_out[i],
                                           d_pool=d_pool, bm=1024, bn=512, seq=seq)
            xs, h = _out_proj((y_pool, c), xs, w_out_bf, mix_post_g[layer], ffn_pre_g[layer],
                              ln=(conv_ln_g[i], conv_ln_b[i]), bm=512)
        else:
            y, w_out_bf = _gated_conv_branch(h, sc_w_in[i], sc_conv_w[i], sc_w_out[i],
                                             bm=1024, bn=512, seq=seq)
            xs, h = _out_proj((y,), xs, w_out_bf, mix_post_g[layer], ffn_pre_g[layer], bm=512)
        a, w2_bf = _ffn_up(h, ffn_w1, ffn_w2, layer, bm=2048, bn=1024)
        g_next = mix_pre_g[layer + 1] if layer + 1 < depth and (layer + 1) % 2 == 1 else None
        xs, h = _ffn_down(a, w2_bf, xs, ffn_post_g[layer], g_next, bm=256)
    return xs.reshape(batch, seq, d)
```

```python
import functools

import jax
import jax.numpy as jnp
from jax import lax
from jax.experimental import pallas as pl
from jax.experimental.pallas import tpu as pltpu

EPS = 1e-6
POOL_WINDOWS = (2, 4, 8, 16)

V7X_SUBLANES = 8
V7X_LANES = 128
V7X_VMEM_LIMIT_CAP = 60000 * 1024

F32 = jnp.float32
BF16 = jnp.bfloat16


def _halo_rows(n):
    return -(-n // V7X_SUBLANES) * V7X_SUBLANES


def _params(semantics, vmem_bytes):
    assert vmem_bytes <= 1.35 * V7X_VMEM_LIMIT_CAP
    return pltpu.CompilerParams(
        dimension_semantics=semantics,
        vmem_limit_bytes=V7X_VMEM_LIMIT_CAP)


def _const_spec(shape, index_map):
    return pl.BlockSpec(shape, index_map, pipeline_mode=pl.Buffered(1))


def _cast_weight(dst, src, *, col0=0, rows=256):
    n = src.shape[0] // rows
    cols = slice(col0, col0 + src.shape[1])

    def body(c, carry):
        r = pl.multiple_of(c * rows, rows)
        dst[pl.ds(r, rows), cols] = src[pl.ds(r, rows), :].astype(BF16)
        return carry

    lax.fori_loop(0, n, body, 0)


def _rms(x, g):
    ms = jnp.mean(x * x, axis=-1, keepdims=True)
    return x * lax.rsqrt(ms + EPS) * g


def _load_history(buf, i, *, bm, seq, halo):
    first = lax.rem(i * bm, seq) == 0

    @pl.when(first)
    def _():
        buf[..., 0:halo, :] = jnp.zeros(buf.shape[:-2] + (halo, buf.shape[-1]), buf.dtype)

    @pl.when(jnp.logical_not(first))
    def _():
        buf[..., 0:halo, :] = buf[..., bm:bm + halo, :]


def _pool_kernel(x_ref, g_ref, w_ref, pw_ref, ps_ref, o_ref, h_ref, wbf, pwbf, ubuf,
                 *, bm, seq, halo, group, sub):
    i = pl.program_id(0)

    @pl.when(i == 0)
    def _():
        _cast_weight(wbf, w_ref)
        pwbf[...] = pw_ref[...].astype(BF16)

    _load_history(ubuf, i, bm=bm, seq=seq, halo=halo)
    for s0 in range(0, bm, sub):
        rows = slice(s0, s0 + sub)
        h = _rms(x_ref[rows, :], g_ref[...]).astype(BF16)
        h_ref[rows, :] = h
        ubuf[halo + s0:halo + s0 + sub, :] = jnp.dot(h, wbf[...], preferred_element_type=F32)

        pos = lax.rem(i * bm, seq) + s0 + lax.broadcasted_iota(jnp.int32, (sub, 1), 0)
        for g, w in enumerate(POOL_WINDOWS):
            cols = slice(g * group, (g + 1) * group)
            s = ubuf[s0:s0 + halo + sub, cols]
            span = 1
            while span < w:
                s = s + pltpu.roll(s, span, 0)
                span *= 2
            u = ubuf[halo + s0:halo + s0 + sub, cols]
            cnt = jnp.minimum(pos + 1, w).astype(F32)
            pooled = s[halo:, :] / cnt - u
            mixed = jnp.dot(pooled.astype(BF16), pwbf[g], preferred_element_type=F32)
            o_ref[rows, cols] = (mixed * ps_ref[:, cols]).astype(o_ref.dtype)


def _pool_branch(x, g_pre, w_in, pool_w, pool_scale, *, bm, seq):
    m, d = x.shape
    n_groups, group, _ = pool_w.shape
    d_pool = n_groups * group
    halo = _halo_rows(max(POOL_WINDOWS) - 1)
    assert all(w & (w - 1) == 0 for w in POOL_WINDOWS)
    assert seq % bm == 0 and m % seq == 0
    kern = functools.partial(_pool_kernel, bm=bm, seq=seq, halo=halo, group=group,
                             sub=min(bm, 256))
    vmem = (d * d_pool * (4 + 2) + 2 * bm * d * (4 + 2) + 2 * bm * d_pool * 2
            + (halo + bm) * d_pool * 4 + 3 * bm * d_pool * 4 + pool_w.size * 6 + 3 * bm * d * 4)
    return pl.pallas_call(
        kern,
        grid=(m // bm,),
        in_specs=[pl.BlockSpec((bm, d), lambda i: (i, 0)),
                  _const_spec((1, d), lambda i: (0, 0)),
                  _const_spec((d, d_pool), lambda i: (0, 0)),
                  _const_spec(pool_w.shape, lambda i: (0, 0, 0)),
                  _const_spec((1, d_pool), lambda i: (0, 0))],
        out_specs=[pl.BlockSpec((bm, d_pool), lambda i: (i, 0)),
                   pl.BlockSpec((bm, d), lambda i: (i, 0))],
        out_shape=[jax.ShapeDtypeStruct((m, d_pool), BF16),
                   jax.ShapeDtypeStruct((m, d), BF16)],
        scratch_shapes=[pltpu.VMEM((d, d_pool), BF16),
                        pltpu.VMEM(pool_w.shape, BF16),
                        pltpu.VMEM((halo + bm, d_pool), F32)],
        compiler_params=_params(("arbitrary",), vmem),
        name="pool_branch",
    )(x, g_pre.reshape(1, d), w_in, pool_w, pool_scale.reshape(1, d_pool))


def _cast_slab_specs(rows, d, ni):
    index = lambda j, i: (j * ni + i, 0)
    return pl.BlockSpec((rows, d), index), pl.BlockSpec((rows, d), index)


def _glu_conv_kernel(h_ref, wv_ref, wg_ref, cw_ref, cb_ref, wo_ref, o_ref, wobf_ref,
                     wall, gbuf, wrap, obuf, *, bm, bn, seq, halo, taps, sub):
    i = pl.program_id(1)

    @pl.when(i == 0)
    def _():
        _cast_weight(wall, wv_ref, col0=0)
        _cast_weight(wall, wg_ref, col0=bn)

    wobf_ref[...] = wo_ref[...].astype(BF16)
    _load_history(gbuf, i, bm=bm, seq=seq, halo=halo)

    n_lane_tiles = bn // V7X_LANES
    for s0 in range(0, bm, sub):
        z = jnp.dot(h_ref[s0:s0 + sub, :], wall[...], preferred_element_type=F32)
        g = z[:, 0:bn] * jax.nn.sigmoid(z[:, bn:])
        for c in range(n_lane_tiles):
            gbuf[c, halo + s0:halo + s0 + sub, :] = g[:, c * V7X_LANES:(c + 1) * V7X_LANES]

    span = (halo + bm) // V7X_SUBLANES
    first_wrapped = span - (taps - 1)
    for c in range(n_lane_tiles):
        cs = slice(c * V7X_LANES, (c + 1) * V7X_LANES)
        for u in range(first_wrapped, span):
            k = u - first_wrapped
            wrap[c, k * V7X_SUBLANES:(k + 1) * V7X_SUBLANES, :] = pltpu.roll(
                gbuf[c, pl.ds(u, V7X_SUBLANES, stride=span), :], 1, 0)
        for v0 in range(span):
            acc = jnp.broadcast_to(cb_ref[:, cs], (V7X_SUBLANES, V7X_LANES))
            for s in range(taps):
                if v0 >= s:
                    src = gbuf[c, pl.ds(v0 - s, V7X_SUBLANES, stride=span), :]
                else:
                    k = v0 - s + taps - 1
                    src = wrap[c, k * V7X_SUBLANES:(k + 1) * V7X_SUBLANES, :]
                acc = acc + cw_ref[taps - 1 - s:taps - s, cs] * src
            obuf[c, pl.ds(v0, V7X_SUBLANES, stride=span), :] = acc
        o_ref[:, cs] = obuf[c, halo:, :]


def _glu_conv_branch(h, w_in, conv_w, conv_b, w_out, *, d_pool, bm, bn, seq):
    m, d = h.shape
    taps, d_conv = conv_w.shape
    halo = _halo_rows(taps - 1)
    nj, ni = d_conv // bn, m // bm
    v_blk0 = d_pool // bn
    g_blk0 = (d_pool + d_conv) // bn
    ko, do = w_out.shape
    slab = ko // (nj * ni)
    wo_in, wo_out = _cast_slab_specs(slab, do, ni)
    sub = 256
    assert bm % sub == 0 and bn % V7X_LANES == 0 and halo >= taps - 1
    assert seq % bm == 0 and m % seq == 0
    assert ko % (nj * ni) == 0 and d_pool % bn == 0 and d_conv % bn == 0
    kern = functools.partial(_glu_conv_kernel, bm=bm, bn=bn, seq=seq, halo=halo, taps=taps, sub=sub)
    n_lane_tiles = bn // V7X_LANES
    win_shape = (n_lane_tiles, halo + bm, V7X_LANES)
    wrap_shape = (n_lane_tiles, (taps - 1) * V7X_SUBLANES, V7X_LANES)
    vmem = (2 * 2 * d * bn * 4 + 2 * d * bn * 2 + 2 * bm * d * 2 + 2 * bm * bn * 4
            + 2 * (halo + bm) * bn * 4 + 4 * sub * bn * 4 + 2 * slab * do * 6
            + 4 * wrap_shape[0] * wrap_shape[1] * wrap_shape[2])
    return pl.pallas_call(
        kern,
        grid=(nj, ni),
        in_specs=[pl.BlockSpec((bm, d), lambda j, i: (i, 0)),
                  pl.BlockSpec((d, bn), lambda j, i: (0, v_blk0 + j)),
                  pl.BlockSpec((d, bn), lambda j, i: (0, g_blk0 + j)),
                  pl.BlockSpec((taps, bn), lambda j, i: (0, j)),
                  pl.BlockSpec((1, bn), lambda j, i: (0, j)),
                  wo_in],
        out_specs=[pl.BlockSpec((bm, bn), lambda j, i: (i, j)), wo_out],
        out_shape=[jax.ShapeDtypeStruct((m, d_conv), F32),
                   jax.ShapeDtypeStruct((ko, do), BF16)],
        scratch_shapes=[pltpu.VMEM((d, 2 * bn), BF16),
                        pltpu.VMEM(win_shape, F32),
                        pltpu.VMEM(wrap_shape, F32),
                        pltpu.VMEM(win_shape, F32)],
        compiler_params=_params(("arbitrary", "arbitrary"), vmem),
        name="glu_conv_branch",
    )(h, w_in, w_in, conv_w, conv_b.reshape(1, d_conv), w_out)


def _residual_norms(m, x_ref, gpost_ref, gnext_ref, xo_ref, ho_ref, rows=slice(None)):
    x_new = x_ref[rows, :] + _rms(m, gpost_ref[...])
    xo_ref[rows, :] = x_new
    if ho_ref is not None:
        ho_ref[rows, :] = _rms(x_new, gnext_ref[...]).astype(ho_ref.dtype)


def _lagged_steps(i, n_tiles, produce, consume, produce_first=None):
    @pl.when(i == 0)
    def _():
        (produce_first or produce)()

    @pl.when(jnp.logical_and(i > 0, i < n_tiles))
    def _():
        consume()
        produce()

    @pl.when(i == n_tiles)
    def _():
        consume()


def _out_proj_l0_kernel(yp_ref, c_ref, x_ref, w_ref, lng_ref, lnb_ref, gpost_ref, gnext_ref,
                        xo_ref, ho_ref, *, d_pool, bm, sub):
    for s0 in range(0, bm, sub):
        rows = slice(s0, s0 + sub)
        c = c_ref[rows, :]
        mu = jnp.mean(c, axis=-1, keepdims=True)
        cc = c - mu
        var = jnp.mean(cc * cc, axis=-1, keepdims=True)
        ln = cc * lax.rsqrt(var + EPS) * lng_ref[...] + lnb_ref[...]
        y_conv = (ln * jax.nn.sigmoid(ln)).astype(BF16)
        m = jnp.dot(yp_ref[rows, :], w_ref[0:d_pool, :], preferred_element_type=F32)
        m = m + jnp.dot(y_conv, w_ref[d_pool:, :], preferred_element_type=F32)
        _residual_norms(m, x_ref, gpost_ref, gnext_ref, xo_ref, ho_ref, rows)


def _out_proj_l1_kernel(y_ref, x_ref, w_ref, gpost_ref, gnext_ref, xo_ref, ho_ref, *, bm, sub):
    for s0 in range(0, bm, sub):
        rows = slice(s0, s0 + sub)
        m = jnp.dot(y_ref[rows, :], w_ref[...], preferred_element_type=F32)
        _residual_norms(m, x_ref, gpost_ref, gnext_ref, xo_ref, ho_ref, rows)


def _out_proj(ys, x, w_bf, g_post, g_next, ln=None, *, bm, sub=256):
    assert bm % sub == 0
    m, d = x.shape
    k = w_bf.shape[0]
    row = lambda i: (i, 0)
    fixed = lambda i: (0, 0)
    vec = lambda a: a.reshape(1, -1)
    if ln is None:
        (y,) = ys
        kern = functools.partial(_out_proj_l1_kernel, bm=bm, sub=sub)
        ins = [y, x, w_bf, vec(g_post), vec(g_next)]
        in_specs = [pl.BlockSpec((bm, k), row), pl.BlockSpec((bm, d), row),
                    _const_spec((k, d), fixed), _const_spec((1, d), fixed), _const_spec((1, d), fixed)]
        tile_bytes = bm * k * 2
    else:
        y_pool, c = ys
        d_pool, d_conv = y_pool.shape[1], c.shape[1]
        kern = functools.partial(_out_proj_l0_kernel, d_pool=d_pool, bm=bm, sub=sub)
        ins = [y_pool, c, x, w_bf, vec(ln[0]), vec(ln[1]), vec(g_post), vec(g_next)]
        in_specs = [pl.BlockSpec((bm, d_pool), row), pl.BlockSpec((bm, d_conv), row),
                    pl.BlockSpec((bm, d), row), _const_spec((k, d), fixed),
                    _const_spec((1, d_conv), fixed), _const_spec((1, d_conv), fixed),
                    _const_spec((1, d), fixed), _const_spec((1, d), fixed)]
        tile_bytes = bm * d_pool * 2 + bm * d_conv * 4
    vmem = k * d * 2 + 2 * tile_bytes + 2 * bm * d * (4 + 4 + 2) + 5 * bm * d * 4
    return pl.pallas_call(
        kern,
        grid=(m // bm,),
        in_specs=in_specs,
        out_specs=[pl.BlockSpec((bm, d), row), pl.BlockSpec((bm, d), row)],
        out_shape=[jax.ShapeDtypeStruct((m, d), F32), jax.ShapeDtypeStruct((m, d), BF16)],
        compiler_params=_params(("arbitrary",), vmem),
        name="out_proj_l0" if ln is not None else "out_proj_l1",
    )(*ins)


def _ffn_up_kernel(h_ref, w_ref, w2_ref, o_ref, w2bf_ref, wbf, *, bm, sub):
    @pl.when(pl.program_id(1) == 0)
    def _():
        _cast_weight(wbf, w_ref)

    w2bf_ref[...] = w2_ref[...].astype(BF16)
    for s0 in range(0, bm, sub):
        rows = slice(s0, s0 + sub)
        a = jnp.maximum(jnp.dot(h_ref[rows, :], wbf[...], preferred_element_type=F32), 0.0)
        o_ref[rows, :] = (a * a).astype(o_ref.dtype)


def _ffn_up(h, w1, w2, layer, *, bm, bn):
    m, d = h.shape
    f = w1.shape[2]
    nj, ni = f // bn, m // bm
    slab = f // (nj * ni)
    slab_index = lambda j, i: (j * ni + i, 0)
    sub = min(bm, 512)
    vmem = (2 * d * bn * 4 + d * bn * 2 + 2 * bm * d * 2 + 2 * bm * bn * 2 + 3 * sub * bn * 4
            + 2 * slab * d * 6)
    return pl.pallas_call(
        functools.partial(_ffn_up_kernel, bm=bm, sub=sub),
        grid=(nj, ni),
        in_specs=[pl.BlockSpec((bm, d), lambda j, i: (i, 0)),
                  pl.BlockSpec((None, d, bn), lambda j, i: (layer, 0, j)),
                  pl.BlockSpec((None, slab, d), lambda j, i: (layer, j * ni + i, 0))],
        out_specs=[pl.BlockSpec((bm, bn), lambda j, i: (i, j)),
                   pl.BlockSpec((slab, d), slab_index)],
        out_shape=[jax.ShapeDtypeStruct((m, f), BF16),
                   jax.ShapeDtypeStruct((f, d), BF16)],
        scratch_shapes=[pltpu.VMEM((d, bn), BF16)],
        compiler_params=_params(("arbitrary", "arbitrary"), vmem),
        name="ffn_up",
    )(h, w1, w2)


def _ffn_down_kernel(a_ref, w_hbm, x_ref, gpost_ref, gnext_ref, xo_ref, *rest, with_next, n_tiles,
                     n_chunks):
    ho_ref = rest[0] if with_next else None
    m_buf, w_ref, sems = rest[-3:]
    kc = w_ref.shape[0] // n_chunks

    def chunk_copy(c):
        rows = pl.ds(c * kc, kc)
        return pltpu.make_async_copy(w_hbm.at[rows, :], w_ref.at[rows, :], sems.at[c])

    def first_matmul():
        for c in range(n_chunks):
            chunk_copy(c).start(priority=c % 2)
        for c in range(n_chunks):
            chunk_copy(c).wait()
            part = jnp.dot(a_ref[:, c * kc:(c + 1) * kc], w_ref[c * kc:(c + 1) * kc, :],
                           preferred_element_type=F32)
            if c == 0:
                m_buf[...] = part
            else:
                m_buf[...] += part

    def matmul():
        m_buf[...] = jnp.dot(a_ref[...], w_ref[...], preferred_element_type=F32)

    def norms():
        _residual_norms(m_buf[...], x_ref, gpost_ref, gnext_ref, xo_ref, ho_ref)

    _lagged_steps(pl.program_id(0), n_tiles, matmul, norms, produce_first=first_matmul)


def _ffn_down(a, w2_bf, x, g_post, g_next, *, bm):
    m, f = a.shape
    d = w2_bf.shape[1]
    n_tiles = m // bm
    with_next = g_next is not None
    if not with_next:
        g_next = g_post
    ahead = lambda i: (jnp.minimum(i, n_tiles - 1), 0)
    behind = lambda i: (jnp.maximum(i - 1, 0), 0)
    fixed = lambda i: (0, 0)
    out_specs = [pl.BlockSpec((bm, d), behind)]
    out_shape = [jax.ShapeDtypeStruct((m, d), F32)]
    if with_next:
        out_specs.append(pl.BlockSpec((bm, d), behind))
        out_shape.append(jax.ShapeDtypeStruct((m, d), BF16))
    vmem = f * d * 2 + 2 * bm * f * 2 + 2 * bm * d * (4 + 4 + 2) + 4 * bm * d * 4
    n_chunks = 8
    assert f % n_chunks == 0
    outs = pl.pallas_call(
        functools.partial(_ffn_down_kernel, with_next=with_next, n_tiles=n_tiles,
                          n_chunks=n_chunks),
        grid=(n_tiles + 1,),
        in_specs=[pl.BlockSpec((bm, f), ahead),
                  pl.BlockSpec(memory_space=pl.ANY),
                  pl.BlockSpec((bm, d), behind),
                  _const_spec((1, d), fixed), _const_spec((1, d), fixed)],
        out_specs=out_specs,
        out_shape=out_shape,
        scratch_shapes=[pltpu.VMEM((bm, d), F32),
                        pltpu.VMEM((f, d), BF16),
                        pltpu.SemaphoreType.DMA((n_chunks,))],
        compiler_params=_params(("arbitrary",), vmem),
        name="ffn_down",
    )(a, w2_bf, x, g_post.reshape(1, d), g_next.reshape(1, d))
    return outs if with_next else (outs[0], None)


def _gated_conv_kernel(h_ref, wb_ref, wc_ref, wu_ref, cw_ref, wo_ref, o_ref, wobf_ref,
                       wall, pbuf, *, bm, bn, seq, halo, taps, sub):
    i = pl.program_id(1)

    @pl.when(i == 0)
    def _():
        _cast_weight(wall, wc_ref, col0=0)
        _cast_weight(wall, wu_ref, col0=bn)
        _cast_weight(wall, wb_ref, col0=2 * bn)

    wobf_ref[...] = wo_ref[...].astype(BF16)
    _load_history(pbuf, i, bm=bm, seq=seq, halo=halo)
    cw = cw_ref[...]
    for s0 in range(0, bm, sub):
        z = jnp.dot(h_ref[s0:s0 + sub, :], wall[...], preferred_element_type=F32)
        pbuf[halo + s0:halo + s0 + sub, :] = z[:, 0:bn] * z[:, bn:2 * bn]
        conv = cw[taps - 1:taps, :] * pbuf[halo + s0:halo + s0 + sub, :]
        for k in range(taps - 1):
            start = halo + s0 - (taps - 1) + k
            conv = conv + cw[k:k + 1, :] * pbuf[start:start + sub, :]
        o_ref[s0:s0 + sub, :] = (z[:, 2 * bn:] * conv).astype(o_ref.dtype)


def _gated_conv_branch(h, w_in, conv_w, w_out, *, bm, bn, seq):
    m, d = h.shape
    taps, d_short = conv_w.shape
    halo = _halo_rows(taps - 1)
    nj, ni = d_short // bn, m // bm
    ko, do = w_out.shape
    slab = ko // (nj * ni)
    wo_in, wo_out = _cast_slab_specs(slab, do, ni)
    assert seq % bm == 0 and m % seq == 0
    assert ko % (nj * ni) == 0 and d_short % bn == 0
    kern = functools.partial(_gated_conv_kernel, bm=bm, bn=bn, seq=seq, halo=halo, taps=taps,
                             sub=min(bm, 512))
    vmem = (3 * 2 * d * bn * 4 + 3 * d * bn * 2 + 2 * bm * d * 2 + 2 * bm * bn * 2
            + (halo + bm) * bn * 4 + 5 * bm * bn * 4 + 2 * slab * do * 6)
    return pl.pallas_call(
        kern,
        grid=(nj, ni),
        in_specs=[pl.BlockSpec((bm, d), lambda j, i: (i, 0)),
                  pl.BlockSpec((d, bn), lambda j, i: (0, j)),
                  pl.BlockSpec((d, bn), lambda j, i: (0, nj + j)),
                  pl.BlockSpec((d, bn), lambda j, i: (0, 2 * nj + j)),
                  pl.BlockSpec((taps, bn), lambda j, i: (0, j)),
                  wo_in],
        out_specs=[pl.BlockSpec((bm, bn), lambda j, i: (i, j)), wo_out],
        out_shape=[jax.ShapeDtypeStruct((m, d_short), BF16),
                   jax.ShapeDtypeStruct((ko, do), BF16)],
        scratch_shapes=[pltpu.VMEM((d, 3 * bn), BF16), pltpu.VMEM((halo + bm, bn), F32)],
        compiler_params=_params(("arbitrary", "arbitrary"), vmem),
        name="gated_conv_branch",
    )(h, w_in, w_in, w_in, conv_w, w_out)


def kernel(x, mix_pre_g, mix_post_g, ffn_pre_g, ffn_post_g, ab_w_in, pool_w, pool_scale,
           conv_w, conv_b, conv_ln_g, conv_ln_b, ab_w_out, sc_w_in, sc_conv_w, sc_w_out,
           ffn_w1, ffn_w2):
    batch, seq, d = x.shape
    depth = mix_pre_g.shape[0]
    d_pool = pool_scale.shape[1]
    xs = x.reshape(batch * seq, d)
    h = None
    for layer in range(depth):
        i = layer // 2
        if layer % 2 == 0:
            y_pool, h = _pool_branch(xs, mix_pre_g[layer], ab_w_in[i], pool_w[i], pool_scale[i],
                                     bm=512, seq=seq)
            c, w_out_bf = _glu_conv_branch(h, ab_w_in[i], conv_w[i], conv_b[i], ab_w_out[i],
                                           d_pool=d_pool, bm=1024, bn=512, seq=seq)
            xs, h = _out_proj((y_pool, c), xs, w_out_bf, mix_post_g[layer], ffn_pre_g[layer],
                              ln=(conv_ln_g[i], conv_ln_b[i]), bm=512)
        else:
            y, w_out_bf = _gated_conv_branch(h, sc_w_in[i], sc_conv_w[i], sc_w_out[i],
                                             bm=1024, bn=512, seq=seq)
            xs, h = _out_proj((y,), xs, w_out_bf, mix_post_g[layer], ffn_pre_g[layer], bm=512)
        a, w2_bf = _ffn_up(h, ffn_w1, ffn_w2, layer, bm=2048, bn=1024)
        g_next = mix_pre_g[layer + 1] if layer + 1 < depth and (layer + 1) % 2 == 1 else None
        xs, h = _ffn_down(a, w2_bf, xs, ffn_post_g[layer], g_next, bm=256)
    return xs.reshape(batch, seq, d)
```
